```python
import jax, jax.numpy as jnp
from jax import lax
import numpy as np

D_MODEL = 1024
BATCH = 2
SEQ = 8192
DEPTH = 1
DEC_BATCH = 32
DEC_SEQ = 4
PAST_LEN = 8192
PAGE_SIZE = 128

POOL_WIDTH = D_MODEL // 2
POOL_WINDOWS = (2, 4, 8, 16)
POOL_GROUP = POOL_WIDTH // len(POOL_WINDOWS)
POOL_HIST = max(POOL_WINDOWS) - 1
ATTN_WIDTH = D_MODEL - POOL_WIDTH
HEAD_DIM = 64
N_HEADS = ATTN_WIDTH // HEAD_DIM
IDX_HEADS = 4
IDX_DIM = 64
TOPK_MAX = 256
Q_BLOCK = 128
D_FF = 4 * D_MODEL
NORM_EPS = 1e-6
NEG_INF = -1e30
IN_SPLITS = (POOL_WIDTH, ATTN_WIDTH, ATTN_WIDTH, ATTN_WIDTH,
             IDX_HEADS * IDX_DIM, IDX_DIM, IDX_HEADS)
IN_WIDTH = sum(IN_SPLITS)

kernel_name = 'hymba_pool_dsa_hybrid_step'


def rmsnorm(x, g):
    xf = x.astype(jnp.float32)
    out = xf * lax.rsqrt(jnp.mean(xf * xf, axis=-1, keepdims=True) + NORM_EPS) * g.astype(jnp.float32)
    return out.astype(x.dtype)


def project(x, ln, w_in):
    b, t, _ = x.shape
    z = jnp.einsum('btd,de->bte', rmsnorm(x, ln), w_in)
    offs = np.cumsum((0,) + IN_SPLITS)
    u, q, k, v, qi, ki, wi = [z[..., offs[i]:offs[i + 1]] for i in range(len(IN_SPLITS))]
    q = q.reshape(b, t, N_HEADS, HEAD_DIM)
    k = k.reshape(b, t, N_HEADS, HEAD_DIM)
    v = v.reshape(b, t, N_HEADS, HEAD_DIM)
    qi = qi.reshape(b, t, IDX_HEADS, IDX_DIM)
    return u, q, k, v, qi, ki, wi


def pool_mix(u_hist, u_new, start_pos, w_pool, pool_scale):
    t = u_new.shape[1]
    u_ext = jnp.concatenate([u_hist.astype(u_new.dtype), u_new], axis=1).astype(jnp.float32)
    cs = jnp.cumsum(jnp.pad(u_ext, ((0, 0), (1, 0), (0, 0))), axis=1)
    cs_end = cs[:, POOL_HIST + 1:]
    u_cur = u_ext[:, POOL_HIST:]
    pos = start_pos + jnp.arange(t)
    outs = []
    for g, w in enumerate(POOL_WINDOWS):
        sl = slice(g * POOL_GROUP, (g + 1) * POOL_GROUP)
        cs_start = cs[:, POOL_HIST + 1 - w:POOL_HIST + 1 - w + t, sl]
        cnt = jnp.minimum(pos + 1, w).astype(jnp.float32)[None, :, None]
        pooled = (cs_end[..., sl] - cs_start) / cnt - u_cur[..., sl]
        outs.append(jnp.einsum('btc,cd->btd', pooled, w_pool[g].astype(jnp.float32)))
    y = jnp.concatenate(outs, axis=-1) * pool_scale.astype(jnp.float32)
    return y.astype(u_new.dtype)


def indexer_select(qi, wi, ki_all, q_pos, topk):
    n_keys = ki_all.shape[1]
    s = jnp.einsum('bqhd,bld->bqhl', qi.astype(jnp.float32), ki_all.astype(jnp.float32)) * (IDX_DIM ** -0.5)
    score = jnp.einsum('bqhl,bqh->bql', jax.nn.relu(s), wi.astype(jnp.float32)) * (IDX_HEADS ** -0.5)
    admissible = jnp.arange(n_keys)[None, None, :] <= q_pos[None, :, None]
    score = jnp.where(admissible, score, NEG_INF)
    _, idx = lax.top_k(score, topk)
    valid = idx <= q_pos[None, :, None]
    return idx, valid


def attend(q, kg, vg, valid):
    b, nq = q.shape[:2]
    logits = jnp.einsum('bqhd,bqkhd->bqhk', q.astype(jnp.float32), kg.astype(jnp.float32)) * (HEAD_DIM ** -0.5)
    logits = jnp.where(valid[:, :, None, :], logits, NEG_INF)
    p = jax.nn.softmax(logits, axis=-1)
    out = jnp.einsum('bqhk,bqkhd->bqhd', p, vg.astype(jnp.float32))
    return out.reshape(b, nq, N_HEADS * HEAD_DIM).astype(q.dtype)


def gather_rows(rows, idx):
    return jax.vmap(lambda r, i: r[i])(rows, idx)


def prompt_sparse_attention(q, k, v, qi, ki, wi):
    b, t = q.shape[:2]
    topk = min(TOPK_MAX, t // 4)
    nb = t // Q_BLOCK

    def to_blocks(a):
        return jnp.moveaxis(a.reshape((b, nb, Q_BLOCK) + a.shape[2:]), 1, 0)

    def one_block(args):
        qb, qib, wib, j = args
        q_pos = j * Q_BLOCK + jnp.arange(Q_BLOCK)
        idx, valid = indexer_select(qib, wib, ki, q_pos, topk)
        return attend(qb, gather_rows(k, idx), gather_rows(v, idx), valid)

    out = lax.map(one_block, (to_blocks(q), to_blocks(qi), to_blocks(wi), jnp.arange(nb)))
    return jnp.moveaxis(out, 0, 1).reshape(b, t, ATTN_WIDTH)


def sample_sparse_attention(q, k, v, qi, ki, wi, cache_k, cache_v, cache_kidx, page_table, layer):
    db, t = q.shape[:2]
    n_pages = page_table.shape[1]
    past = n_pages * PAGE_SIZE
    ki_past = cache_kidx[layer, page_table].reshape(db, past, IDX_DIM).astype(ki.dtype)
    ki_all = jnp.concatenate([ki_past, ki], axis=1)
    topk = min(TOPK_MAX, (past + t) // 4)
    q_pos = past + jnp.arange(t)
    idx, valid = indexer_select(qi, wi, ki_all, q_pos, topk)
    in_past = (idx < past)[..., None, None]
    pidx = jnp.minimum(idx, past - 1)
    phys = jax.vmap(lambda pt, i: pt[i])(page_table, pidx // PAGE_SIZE)
    off = pidx % PAGE_SIZE
    nidx = jnp.clip(idx - past, 0, t - 1)
    kg = jnp.where(in_past, cache_k[layer, phys, off].astype(k.dtype), gather_rows(k, nidx))
    vg = jnp.where(in_past, cache_v[layer, phys, off].astype(v.dtype), gather_rows(v, nidx))
    return attend(q, kg, vg, valid)


def finish_layer(x, pool_out, attn_out, w_out, ln2, w_ff1, w_ff2):
    h = x + jnp.einsum('bte,ed->btd', jnp.concatenate([pool_out, attn_out], axis=-1), w_out)
    a = jnp.square(jax.nn.relu(jnp.einsum('btd,df->btf', rmsnorm(h, ln2), w_ff1)))
    return h + jnp.einsum('btf,fd->btd', a, w_ff2)


def setup_inputs(seed: int = 0) -> dict:
    key = jax.random.key(seed)
    ks = jax.random.split(key, 20)
    n_pages = PAST_LEN // PAGE_SIZE
    n_used = DEC_BATCH * n_pages
    n_pool = (n_used * 5) // 4
    f32 = jnp.float32
    nrm = lambda k, s, sc=1.0: jax.random.normal(k, s, f32) * sc
    page_table = jax.random.permutation(ks[7], n_pool)[:n_used].reshape(DEC_BATCH, n_pages).astype(jnp.int32)
    return {
        'x_prompt': nrm(ks[0], (BATCH, SEQ, D_MODEL)),
        'x_sample': nrm(ks[1], (DEC_BATCH, DEC_SEQ, D_MODEL)),
        'cache_k': nrm(ks[2], (DEPTH, n_pool, PAGE_SIZE, N_HEADS, HEAD_DIM)),
        'cache_v': nrm(ks[3], (DEPTH, n_pool, PAGE_SIZE, N_HEADS, HEAD_DIM)),
        'cache_kidx': nrm(ks[4], (DEPTH, n_pool, PAGE_SIZE, IDX_DIM)),
        'state_pool': nrm(ks[5], (DEPTH, DEC_BATCH, POOL_HIST, POOL_WIDTH)),
        'page_table': page_table,
        'ln1': 1.0 + nrm(ks[8], (DEPTH, D_MODEL), 0.1),
        'w_in': nrm(ks[9], (DEPTH, D_MODEL, IN_WIDTH), D_MODEL ** -0.5),
        'w_pool': nrm(ks[10], (DEPTH, len(POOL_WINDOWS), POOL_GROUP, POOL_GROUP), POOL_GROUP ** -0.5),
        'pool_scale': 1.0 + nrm(ks[11], (DEPTH, POOL_WIDTH), 0.1),
        'w_out': nrm(ks[12], (DEPTH, D_MODEL, D_MODEL), D_MODEL ** -0.5),
        'ln2': 1.0 + nrm(ks[13], (DEPTH, D_MODEL), 0.1),
        'w_ff1': nrm(ks[14], (DEPTH, D_MODEL, D_FF), D_MODEL ** -0.5),
        'w_ff2': nrm(ks[15], (DEPTH, D_FF, D_MODEL), D_FF ** -0.5),
        'ln_f': 1.0 + nrm(ks[16], (D_MODEL,), 0.1),
    }


def reference(x_prompt, x_sample, cache_k, cache_v, cache_kidx, state_pool, page_table,
              ln1, w_in, w_pool, pool_scale, w_out, ln2, w_ff1, w_ff2, ln_f):
    yp, ys = x_prompt, x_sample
    kp_l, vp_l, kip_l, pp_l = [], [], [], []
    ks_l, vs_l, kis_l, ps_l = [], [], [], []
    for l in range(DEPTH):
        u, q, k, v, qi, ki, wi = project(yp, ln1[l], w_in[l])
        hist0 = jnp.zeros((u.shape[0], POOL_HIST, POOL_WIDTH), u.dtype)
        pool_out = pool_mix(hist0, u, 0, w_pool[l], pool_scale[l])
        attn_out = prompt_sparse_attention(q, k, v, qi, ki, wi)
        yp = finish_layer(yp, pool_out, attn_out, w_out[l], ln2[l], w_ff1[l], w_ff2[l])
        kp_l.append(k); vp_l.append(v); kip_l.append(ki); pp_l.append(u[:, -POOL_HIST:])
        u, q, k, v, qi, ki, wi = project(ys, ln1[l], w_in[l])
        hist = state_pool[l].astype(u.dtype)
        pool_out = pool_mix(hist, u, PAST_LEN, w_pool[l], pool_scale[l])
        attn_out = sample_sparse_attention(q, k, v, qi, ki, wi, cache_k, cache_v, cache_kidx, page_table, l)
        ys = finish_layer(ys, pool_out, attn_out, w_out[l], ln2[l], w_ff1[l], w_ff2[l])
        ks_l.append(k); vs_l.append(v); kis_l.append(ki)
        ps_l.append(jnp.concatenate([hist, u], axis=1)[:, -POOL_HIST:])
    y_prompt = rmsnorm(yp, ln_f)
    y_sample = rmsnorm(ys, ln_f)
    return (y_prompt, y_sample,
            jnp.stack(kp_l), jnp.stack(vp_l), jnp.stack(kip_l), jnp.stack(pp_l),
            jnp.stack(ks_l), jnp.stack(vs_l), jnp.stack(kis_l), jnp.stack(ps_l))
```

```python
import functools

import jax
import jax.numpy as jnp
from jax import lax
from jax.experimental import pallas as pl
from jax.experimental.pallas import tpu as pltpu

D_MODEL = 1024
POOL_WIDTH = 512
POOL_WINDOWS = (2, 4, 8, 16)
POOL_GROUP = 128
POOL_HIST = 15
ATTN_WIDTH = 512
HEAD_DIM = 64
N_HEADS = 8
IDX_HEADS = 4
IDX_DIM = 64
TOPK_MAX = 256
PAGE_SIZE = 128
D_FF = 4096
NORM_EPS = 1e-6
NEG_INF = -1e30
IN_WIDTH = 2372
IN_PAD = 2432
KW_OFF = 2304

LANES = 128
ROW_TILE = 512
FINISH_TILE = 256
Q_TILE = 128
KEY_CHUNK = 512
BLOCKS_PER_CHUNK = KEY_CHUNK // LANES
PAGES_PER_STEP = 8
SAMPLE_ROWS = 8
VMEM_LIMIT = 56 * 1024 * 1024

F32 = jnp.float32
BF16 = jnp.bfloat16


def _rmsnorm(x, g):
    return x * lax.rsqrt(jnp.mean(x * x, axis=-1, keepdims=True) + NORM_EPS) * g


def _proj_kernel(x_ref, ln_ref, w_ref, u_ref, k_ref, v_ref, ki_ref, kw_ref,
                 qb_ref, kt_ref, vb_ref, qib_ref, kit_ref):
    xn = _rmsnorm(x_ref[...], ln_ref[...]).astype(BF16)
    z = jnp.dot(xn, w_ref[...], preferred_element_type=F32)
    u_ref[...] = z[:, 0:512]
    q = z[:, 512:1024]
    k = z[:, 1024:1536]
    v = z[:, 1536:2048]
    kw = z[:, KW_OFF:IN_PAD]
    k_ref[...] = k
    v_ref[...] = v
    ki_ref[...] = kw[:, :IDX_DIM]
    kw_ref[...] = kw
    qb_ref[...] = (q * (HEAD_DIM ** -0.5)).astype(BF16)
    qib_ref[...] = (z[:, 2048:KW_OFF] * (IDX_DIM ** -0.5)).astype(BF16)
    vb_ref[...] = v.astype(BF16)
    kt_ref[0] = k.T.astype(BF16)
    kit_ref[0] = kw.T[:IDX_DIM, :].astype(BF16)


def _project(x2d, ln, w_bf, tm):
    n = x2d.shape[0]
    nt = n // tm
    row = lambda w: pl.BlockSpec((tm, w), lambda i: (i, 0))
    out_shape = (
        jax.ShapeDtypeStruct((n, POOL_WIDTH), F32),
        jax.ShapeDtypeStruct((n, ATTN_WIDTH), F32),
        jax.ShapeDtypeStruct((n, ATTN_WIDTH), F32),
        jax.ShapeDtypeStruct((n, IDX_DIM), F32),
        jax.ShapeDtypeStruct((n, LANES), F32),
        jax.ShapeDtypeStruct((n, ATTN_WIDTH), BF16),
        jax.ShapeDtypeStruct((nt, ATTN_WIDTH, tm), BF16),
        jax.ShapeDtypeStruct((n, ATTN_WIDTH), BF16),
        jax.ShapeDtypeStruct((n, IDX_HEADS * IDX_DIM), BF16),
        jax.ShapeDtypeStruct((nt, IDX_DIM, tm), BF16),
    )
    out_specs = (
        row(POOL_WIDTH), row(ATTN_WIDTH), row(ATTN_WIDTH), row(IDX_DIM), row(LANES),
        row(ATTN_WIDTH),
        pl.BlockSpec((1, ATTN_WIDTH, tm), lambda i: (i, 0, 0)),
        row(ATTN_WIDTH), row(IDX_HEADS * IDX_DIM),
        pl.BlockSpec((1, IDX_DIM, tm), lambda i: (i, 0, 0)),
    )
    return pl.pallas_call(
        _proj_kernel,
        grid=(nt,),
        in_specs=[row(D_MODEL),
                  pl.BlockSpec((1, D_MODEL), lambda i: (0, 0)),
                  pl.BlockSpec((D_MODEL, IN_PAD), lambda i: (0, 0))],
        out_specs=out_specs,
        out_shape=out_shape,
        compiler_params=pltpu.CompilerParams(
            dimension_semantics=("arbitrary",), vmem_limit_bytes=VMEM_LIMIT),
        name="proj",
    )(x2d, ln, w_bf)


def _pool_prompt_kernel(u_ref, halo_ref, o_ref, ext_ref, *, tm, tiles_per_seq):
    it = pl.program_id(0) % tiles_per_seq
    ext_ref[0:16, :] = jnp.where(it == 0, 0.0, halo_ref[...])
    ext_ref[16:, :] = u_ref[...]
    pos = it * tm + lax.broadcasted_iota(jnp.int32, (tm, 1), 0)
    for g, w in enumerate(POOL_WINDOWS):
        cols = slice(g * POOL_GROUP, (g + 1) * POOL_GROUP)
        cur = ext_ref[16:16 + tm, cols]
        s = cur
        for j in range(1, w):
            s = s + ext_ref[16 - j:16 - j + tm, cols]
        cnt = jnp.minimum(pos + 1, w).astype(F32)
        o_ref[:, cols] = (s / cnt - cur).astype(BF16)


def _pool_prompt(u2d, seq_len, tm):
    n = u2d.shape[0]
    hb = tm // 16
    return pl.pallas_call(
        functools.partial(_pool_prompt_kernel, tm=tm, tiles_per_seq=seq_len // tm),
        grid=(n // tm,),
        in_specs=[pl.BlockSpec((tm, POOL_WIDTH), lambda i: (i, 0)),
                  pl.BlockSpec((16, POOL_WIDTH), lambda i: (jnp.maximum(i * hb - 1, 0), 0))],
        out_specs=pl.BlockSpec((tm, POOL_WIDTH), lambda i: (i, 0)),
        out_shape=jax.ShapeDtypeStruct((n, POOL_WIDTH), BF16),
        scratch_shapes=[pltpu.VMEM((tm + 16, POOL_WIDTH), F32)],
        compiler_params=pltpu.CompilerParams(dimension_semantics=("arbitrary",)),
        name="pool_prompt",
    )(u2d, u2d)


def _pool_sample_kernel(u_ref, h_ref, o_ref, *, start_pos):
    t = u_ref.shape[0]

    def row(r, cols):
        return h_ref[POOL_HIST + r, :, cols] if r < 0 else u_ref[r, :, cols]

    for i in range(t):
        for g, w in enumerate(POOL_WINDOWS):
            cols = slice(g * POOL_GROUP, (g + 1) * POOL_GROUP)
            cur = row(i, cols)
            s = cur
            for j in range(1, w):
                s = s + row(i - j, cols)
            cnt = float(min(start_pos + i + 1, w))
            o_ref[i, :, cols] = (s / cnt - cur).astype(BF16)


def _pool_sample(u_t, hist_t, start_pos):
    return pl.pallas_call(
        functools.partial(_pool_sample_kernel, start_pos=start_pos),
        out_shape=jax.ShapeDtypeStruct(u_t.shape, BF16),
        name="pool_sample",
    )(u_t, hist_t)


def _select_to_bias(s_ref, nch, ch, rows, k, idx_bits):
    lane = lax.broadcasted_iota(jnp.int32, (rows, LANES), 1)
    kf = float(k)

    def count(pred):
        def body(c, acc):
            for u in range(ch):
                j = c * ch + u
                acc = acc + jnp.where(pred(s_ref[j], j), 1.0, 0.0)
            return acc
        acc = lax.fori_loop(0, nch, body, jnp.zeros((rows, LANES), F32))
        return jnp.sum(acc, axis=1, keepdims=True)

    def key_to_f32(key):
        bits = key ^ ((key >> 31) & 0x7FFFFFFF)
        return jnp.broadcast_to(lax.bitcast_convert_type(bits, F32), (rows, LANES))

    def value_step(it, tau):
        cand = tau ^ lax.shift_left(jnp.int32(1), 31 - it)
        cb = key_to_f32(cand)
        cnt = count(lambda blk, j: blk >= cb)
        return jnp.where(cnt >= kf, cand, tau)

    tau = lax.fori_loop(0, 32, value_step, jnp.full((rows, 1), -2 ** 31, jnp.int32))
    tb = key_to_f32(tau)
    need = kf - count(lambda blk, j: blk > tb)
    n_ge = count(lambda blk, j: blk >= tb)

    def index_search():
        def index_step(it, p):
            cand = p | lax.shift_left(jnp.int32(1), idx_bits - 1 - it)
            cb = jnp.broadcast_to(cand, (rows, LANES))
            cnt = count(lambda blk, j: (blk == tb) & ((lane + j * LANES) < cb))
            return jnp.where(cnt < need, cand, p)
        return lax.fori_loop(0, idx_bits, index_step, jnp.zeros((rows, 1), jnp.int32))

    any_tie = jnp.max(jnp.where(n_ge > kf, 1.0, 0.0)) > 0.0
    cut = lax.cond(any_tie, index_search,
                   lambda: jnp.full((rows, 1), 2 ** 30, jnp.int32))
    cutb = jnp.broadcast_to(cut, (rows, LANES))

    def write(c, carry):
        for u in range(ch):
            j = c * ch + u
            blk = s_ref[j]
            keep = (blk > tb) | ((blk == tb) & ((lane + j * LANES) <= cutb))
            s_ref[j] = jnp.where(keep, 0.0, NEG_INF)
        return carry
    lax.fori_loop(0, nch, write, 0)


def _softmax_step(h, s, v_h, m_ref, l_ref, acc_ref):
    cols = slice(h * HEAD_DIM, (h + 1) * HEAD_DIM)
    m_old = m_ref[h]
    m_new = jnp.maximum(m_old, jnp.max(s, axis=1, keepdims=True))
    alpha = jnp.exp(m_old - m_new)
    p = jnp.exp(s - m_new)
    l_ref[h] = alpha * l_ref[h] + jnp.sum(p, axis=1, keepdims=True)
    acc_ref[:, cols] = alpha * acc_ref[:, cols] + jnp.dot(
        p.astype(BF16), v_h, preferred_element_type=F32)
    m_ref[h] = m_new


def _softmax_finish(m_ref, l_ref, acc_ref):
    return jnp.concatenate(
        [acc_ref[:, h * HEAD_DIM:(h + 1) * HEAD_DIM] / l_ref[h] for h in range(N_HEADS)], axis=1)


def _attn_prompt_kernel(q_ref, qi_ref, kw_ref, kit_ref, kt_ref, v_ref, o_ref,
                        s_ref, m_ref, l_ref, acc_ref, *, topk):
    qb = pl.program_id(1)
    nch = qb // BLOCKS_PER_CHUNK + 1
    lane = lax.broadcasted_iota(jnp.int32, (Q_TILE, LANES), 1)
    row_pos = qb * Q_TILE + lax.broadcasted_iota(jnp.int32, (Q_TILE, LANES), 0)

    qi = qi_ref[0]
    kw = kw_ref[0]
    w_cols = [kw[:, IDX_DIM + h:IDX_DIM + h + 1] * (IDX_HEADS ** -0.5) for h in range(IDX_HEADS)]

    def score_chunk(c, carry):
        kit = kit_ref[0, c]
        acc = jnp.zeros((Q_TILE, KEY_CHUNK), F32)
        for h in range(IDX_HEADS):
            s = jnp.dot(qi[:, h * IDX_DIM:(h + 1) * IDX_DIM], kit, preferred_element_type=F32)
            acc = acc + w_cols[h] * jnp.maximum(s, 0.0)
        for u in range(BLOCKS_PER_CHUNK):
            j = c * BLOCKS_PER_CHUNK + u
            admissible = (lane + j * LANES) <= row_pos
            s_ref[j] = jnp.where(admissible, acc[:, u * LANES:(u + 1) * LANES], NEG_INF)
        return carry
    lax.fori_loop(0, nch, score_chunk, 0)

    @pl.when((qb + 1) * Q_TILE <= topk)
    def _():
        def write(c, carry):
            for u in range(BLOCKS_PER_CHUNK):
                j = c * BLOCKS_PER_CHUNK + u
                s_ref[j] = jnp.where((lane + j * LANES) <= row_pos, 0.0, NEG_INF)
            return carry
        lax.fori_loop(0, nch, write, 0)

    @pl.when((qb + 1) * Q_TILE > topk)
    def _():
        _select_to_bias(s_ref, nch, BLOCKS_PER_CHUNK, Q_TILE, topk, 13)

    m_ref[...] = jnp.full(m_ref.shape, -jnp.inf, F32)
    l_ref[...] = jnp.zeros(l_ref.shape, F32)
    acc_ref[...] = jnp.zeros(acc_ref.shape, F32)
    q = q_ref[0]

    def attn_chunk(c, carry):
        bias = jnp.concatenate(
            [s_ref[c * BLOCKS_PER_CHUNK + u] for u in range(BLOCKS_PER_CHUNK)], axis=1)
        for h in range(N_HEADS):
            rows = slice(h * HEAD_DIM, (h + 1) * HEAD_DIM)
            s = jnp.dot(q[:, rows], kt_ref[0, c, rows, :], preferred_element_type=F32) + bias
            _softmax_step(h, s, v_ref[0, c, :, rows], m_ref, l_ref, acc_ref)
        return carry
    lax.fori_loop(0, nch, attn_chunk, 0)
    o_ref[0] = _softmax_finish(m_ref, l_ref, acc_ref).astype(BF16)


def _attn_prompt(qb, qib, kw, kit, kt, vb, topk):
    b, t, _ = qb.shape
    nc = t // KEY_CHUNK
    assert t % KEY_CHUNK == 0 and t <= 2 ** 13
    qspec = lambda w: pl.BlockSpec((1, Q_TILE, w), lambda i, j: (i, j, 0))
    full = lambda r, c: pl.BlockSpec((1, nc, r, c), lambda i, j: (i, 0, 0, 0))
    return pl.pallas_call(
        functools.partial(_attn_prompt_kernel, topk=topk),
        grid=(b, t // Q_TILE),
        in_specs=[qspec(ATTN_WIDTH), qspec(IDX_HEADS * IDX_DIM), qspec(LANES),
                  full(IDX_DIM, KEY_CHUNK), full(ATTN_WIDTH, KEY_CHUNK),
                  full(KEY_CHUNK, ATTN_WIDTH)],
        out_specs=qspec(ATTN_WIDTH),
        out_shape=jax.ShapeDtypeStruct((b, t, ATTN_WIDTH), BF16),
        scratch_shapes=[pltpu.VMEM((t // LANES, Q_TILE, LANES), F32),
                        pltpu.VMEM((N_HEADS, Q_TILE, 1), F32),
                        pltpu.VMEM((N_HEADS, Q_TILE, 1), F32),
                        pltpu.VMEM((Q_TILE, ATTN_WIDTH), F32)],
        compiler_params=pltpu.CompilerParams(
            dimension_semantics=("arbitrary", "arbitrary"), vmem_limit_bytes=VMEM_LIMIT),
        name="attn_prompt",
    )(qb, qib, kw, kit, kt, vb)


def _select_sample_kernel(pt_ref, qi_ref, kw_ref, kin_ref, *rest, n_steps, topk):
    pages = rest[:PAGES_PER_STEP]
    o_ref, s_ref, kbuf_ref = rest[PAGES_PER_STEP:]
    g = pl.program_id(1)
    rows = SAMPLE_ROWS
    qi = qi_ref[0].astype(BF16)
    kw = kw_ref[0]
    w_cols = [kw[:, IDX_DIM + h:IDX_DIM + h + 1] * (IDX_HEADS ** -0.5) for h in range(IDX_HEADS)]
    nt =(((1,), (1,)), ((), ()))

    def scores(keys_bf):
        acc = jnp.zeros((rows, keys_bf.shape[0]), F32)
        for h in range(IDX_HEADS):
            s = lax.dot_general(qi[:, h * IDX_DIM:(h + 1) * IDX_DIM], keys_bf, nt,
                                preferred_element_type=F32)
            acc = acc + w_cols[h] * jnp.maximum(s, 0.0)
        return acc

    for j in range(PAGES_PER_STEP):
        kbuf_ref[j * PAGE_SIZE:(j + 1) * PAGE_SIZE, :] = pages[j][0].astype(BF16)
    past = scores(kbuf_ref[...])
    for j in range(PAGES_PER_STEP):
        s_ref[g * PAGES_PER_STEP + j] = past[:, j * PAGE_SIZE:(j + 1) * PAGE_SIZE]

    n_past = n_steps * PAGES_PER_STEP

    @pl.when(g == 0)
    def _():
        new = scores(kin_ref[0].astype(BF16))
        lane = lax.broadcasted_iota(jnp.int32, (rows, LANES), 1)
        row = lax.broadcasted_iota(jnp.int32, (rows, LANES), 0)
        s_ref[n_past] = jnp.where(lane <= row, new, NEG_INF)

    @pl.when(g == n_steps - 1)
    def _():
        _select_to_bias(s_ref, 1, n_past + 1, rows, topk, 14)
        o_ref[0] = s_ref[...]


def _attn_sample_kernel(pt_ref, q_ref, b_ref, kn_ref, vn_ref, *rest):
    kpages = rest[:PAGES_PER_STEP]
    vpages = rest[PAGES_PER_STEP:2 * PAGES_PER_STEP]
    o_ref, kbuf_ref, vbuf_ref, m_ref, l_ref, acc_ref = rest[2 * PAGES_PER_STEP:]
    g = pl.program_id(1)
    n_steps = pl.num_programs(1)
    n_past = b_ref.shape[1] - 1
    q = q_ref[0].astype(BF16)
    nt = (((1,), (1,)), ((), ()))

    def update(k_bf, v_bf, bias):
        for h in range(N_HEADS):
            cols = slice(h * HEAD_DIM, (h + 1) * HEAD_DIM)
            s = lax.dot_general(q[:, cols], k_bf[:, cols], nt, preferred_element_type=F32) + bias
            _softmax_step(h, s, v_bf[:, cols], m_ref, l_ref, acc_ref)

    @pl.when(g == 0)
    def _():
        m_ref[...] = jnp.full(m_ref.shape, -jnp.inf, F32)
        l_ref[...] = jnp.zeros(l_ref.shape, F32)
        acc_ref[...] = jnp.zeros(acc_ref.shape, F32)
        update(kn_ref[0].astype(BF16), vn_ref[0].astype(BF16), b_ref[0, n_past])

    for j in range(PAGES_PER_STEP):
        kbuf_ref[j * PAGE_SIZE:(j + 1) * PAGE_SIZE, :] = kpages[j][0].astype(BF16)
        vbuf_ref[j * PAGE_SIZE:(j + 1) * PAGE_SIZE, :] = vpages[j][0].astype(BF16)
    bias = jnp.concatenate([b_ref[0, g * PAGES_PER_STEP + j] for j in range(PAGES_PER_STEP)], axis=1)
    update(kbuf_ref[...], vbuf_ref[...], bias)

    @pl.when(g == n_steps - 1)
    def _():
        o_ref[0] = _softmax_finish(m_ref, l_ref, acc_ref)


def _page_specs(width):
    def spec(j):
        return pl.BlockSpec((1, PAGE_SIZE, width),
                            lambda b, g, pt: (pt[b, g * PAGES_PER_STEP + j], 0, 0))
    return [spec(j) for j in range(PAGES_PER_STEP)]


def _attn_sample(page_table, q_s, qi_s, kw_s, kin_s, kn_s, vn_s, cache_k, cache_v, cache_kidx, topk):
    db, n_pages = page_table.shape
    assert n_pages % PAGES_PER_STEP == 0
    n_steps = n_pages // PAGES_PER_STEP
    nblk = n_pages + 1
    per_seq = lambda r, w: pl.BlockSpec((1, r, w), lambda b, g, pt: (b, 0, 0))
    bias = pl.pallas_call(
        functools.partial(_select_sample_kernel, n_steps=n_steps, topk=topk),
        grid_spec=pltpu.PrefetchScalarGridSpec(
            num_scalar_prefetch=1,
            grid=(db, n_steps),
            in_specs=[per_seq(SAMPLE_ROWS, IDX_HEADS * IDX_DIM), per_seq(SAMPLE_ROWS, LANES),
                      per_seq(PAGE_SIZE, IDX_DIM)] + _page_specs(IDX_DIM),
            out_specs=pl.BlockSpec((1, nblk, SAMPLE_ROWS, LANES), lambda b, g, pt: (b, 0, 0, 0)),
            scratch_shapes=[pltpu.VMEM((nblk, SAMPLE_ROWS, LANES), F32),
                            pltpu.VMEM((PAGES_PER_STEP * PAGE_SIZE, IDX_DIM), BF16)]),
        out_shape=jax.ShapeDtypeStruct((db, nblk, SAMPLE_ROWS, LANES), F32),
        compiler_params=pltpu.CompilerParams(dimension_semantics=("arbitrary", "arbitrary")),
        name="select_sample",
    )(page_table, qi_s, kw_s, kin_s, *([cache_kidx] * PAGES_PER_STEP))

    return pl.pallas_call(
        _attn_sample_kernel,
        grid_spec=pltpu.PrefetchScalarGridSpec(
            num_scalar_prefetch=1,
            grid=(db, n_steps),
            in_specs=[per_seq(SAMPLE_ROWS, ATTN_WIDTH),
                      pl.BlockSpec((1, nblk, SAMPLE_ROWS, LANES), lambda b, g, pt: (b, 0, 0, 0)),
                      per_seq(PAGE_SIZE, ATTN_WIDTH), per_seq(PAGE_SIZE, ATTN_WIDTH)]
            + _page_specs(ATTN_WIDTH) + _page_specs(ATTN_WIDTH),
            out_specs=per_seq(SAMPLE_ROWS, ATTN_WIDTH),
            scratch_shapes=[pltpu.VMEM((PAGES_PER_STEP * PAGE_SIZE, ATTN_WIDTH), BF16),
                            pltpu.VMEM((PAGES_PER_STEP * PAGE_SIZE, ATTN_WIDTH), BF16),
                            pltpu.VMEM((N_HEADS, SAMPLE_ROWS, 1), F32),
                            pltpu.VMEM((N_HEADS, SAMPLE_ROWS, 1), F32),
                            pltpu.VMEM((SAMPLE_ROWS, ATTN_WIDTH), F32)]),
        out_shape=jax.ShapeDtypeStruct((db, SAMPLE_ROWS, ATTN_WIDTH), F32),
        compiler_params=pltpu.CompilerParams(
            dimension_semantics=("arbitrary", "arbitrary"), vmem_limit_bytes=VMEM_LIMIT),
        name="attn_sample",
    )(page_table, q_s, bias, kn_s, vn_s,
      *([cache_k] * PAGES_PER_STEP), *([cache_v] * PAGES_PER_STEP))


def _finish_kernel(x_ref, pin_ref, attn_ref, wpool_ref, pscale_ref, wout_ref, ln2_ref,
                   w1_ref, w2_ref, lnf_ref, o_ref):
    pool = jnp.concatenate(
        [jnp.dot(pin_ref[:, g * POOL_GROUP:(g + 1) * POOL_GROUP], wpool_ref[g],
                 preferred_element_type=F32) for g in range(len(POOL_WINDOWS))], axis=1)
    pool = (pool * pscale_ref[...]).astype(BF16)
    mixed = jnp.concatenate([pool, attn_ref[...].astype(BF16)], axis=1)
    h = x_ref[...] + jnp.dot(mixed, wout_ref[...], preferred_element_type=F32)
    hn = _rmsnorm(h, ln2_ref[...]).astype(BF16)
    y = h
    fc = 1024
    for c in range(D_FF // fc):
        a = jnp.dot(hn, w1_ref[:, c * fc:(c + 1) * fc], preferred_element_type=F32)
        a = jnp.square(jnp.maximum(a, 0.0)).astype(BF16)
        y = y + jnp.dot(a, w2_ref[c * fc:(c + 1) * fc, :], preferred_element_type=F32)
    o_ref[...] = _rmsnorm(y, lnf_ref[...])


def _finish(x2d, pin, attn, wpool, pscale, wout, ln2, w1, w2, lnf, tm):
    n = x2d.shape[0]
    row = lambda w: pl.BlockSpec((tm, w), lambda i: (i, 0))
    const = lambda shape: pl.BlockSpec(shape, lambda i: (0,) * len(shape))
    return pl.pallas_call(
        _finish_kernel,
        grid=(n // tm,),
        in_specs=[row(D_MODEL), row(POOL_WIDTH), row(ATTN_WIDTH),
                  const(wpool.shape), const(pscale.shape), const(wout.shape), const(ln2.shape),
                  const(w1.shape), const(w2.shape), const(lnf.shape)],
        out_specs=row(D_MODEL),
        out_shape=jax.ShapeDtypeStruct((n, D_MODEL), F32),
        compiler_params=pltpu.CompilerParams(
            dimension_semantics=("arbitrary",), vmem_limit_bytes=VMEM_LIMIT),
        name="finish",
    )(x2d, pin, attn, wpool, pscale, wout, ln2, w1, w2, lnf)


def kernel(x_prompt, x_sample, cache_k, cache_v, cache_kidx, state_pool, page_table,
           ln1, w_in, w_pool, pool_scale, w_out, ln2, w_ff1, w_ff2, ln_f):
    depth = ln1.shape[0]
    assert depth == 1
    b, t, d = x_prompt.shape
    db, dt, _ = x_sample.shape
    n_pool = cache_k.shape[1]
    past = page_table.shape[1] * PAGE_SIZE
    l = 0

    w_in_bf = jnp.pad(w_in[l], ((0, 0), (0, IN_PAD - IN_WIDTH))).astype(BF16)
    wpool_bf = w_pool[l].astype(BF16)
    wout_bf = w_out[l].astype(BF16)
    w1_bf = w_ff1[l].astype(BF16)
    w2_bf = w_ff2[l].astype(BF16)
    ln1_l = ln1[l].reshape(1, d)
    ln2_l = ln2[l].reshape(1, d)
    lnf_l = ln_f.reshape(1, d)
    pscale_l = pool_scale[l].reshape(1, POOL_WIDTH)

    xp = x_prompt.reshape(b * t, d)
    u, k, v, ki, kw, qb, kt, vb, qib, kit = _project(xp, ln1_l, w_in_bf, ROW_TILE)
    pin = _pool_prompt(u, t, ROW_TILE)
    nc = t // KEY_CHUNK
    attn = _attn_prompt(
        qb.reshape(b, t, ATTN_WIDTH), qib.reshape(b, t, IDX_HEADS * IDX_DIM),
        kw.reshape(b, t, LANES), kit.reshape(b, nc, IDX_DIM, KEY_CHUNK),
        kt.reshape(b, nc, ATTN_WIDTH, KEY_CHUNK), vb.reshape(b, nc, KEY_CHUNK, ATTN_WIDTH),
        min(TOPK_MAX, t // 4))
    y_prompt = _finish(xp, pin, attn.reshape(b * t, ATTN_WIDTH), wpool_bf, pscale_l, wout_bf,
                       ln2_l, w1_bf, w2_bf, lnf_l, FINISH_TILE).reshape(b, t, d)
    new_k_prompt = k.reshape(1, b, t, N_HEADS, HEAD_DIM)
    new_v_prompt = v.reshape(1, b, t, N_HEADS, HEAD_DIM)
    new_kidx_prompt = ki.reshape(1, b, t, IDX_DIM)
    new_pool_prompt = u.reshape(b, t, POOL_WIDTH)[:, t - POOL_HIST:][None]

    ns = db * dt
    xs = x_sample.reshape(ns, d)
    u_s, k_s, v_s, ki_s, kw_s, qb_s, _, _, qib_s, _ = _project(xs, ln1_l, w_in_bf, ns)
    hist = state_pool[l]
    u_s3 = u_s.reshape(db, dt, POOL_WIDTH)
    pin_s = _pool_sample(jnp.swapaxes(u_s3, 0, 1), jnp.swapaxes(hist, 0, 1), past)
    pin_s = jnp.swapaxes(pin_s, 0, 1).reshape(ns, POOL_WIDTH)

    def pad_rows(a, rows):
        a = a.reshape(db, dt, a.shape[-1]).astype(F32)
        return jnp.pad(a, ((0, 0), (0, rows - dt), (0, 0)))

    attn_s = _attn_sample(
        page_table,
        pad_rows(qb_s, SAMPLE_ROWS),
        pad_rows(qib_s, SAMPLE_ROWS),
        pad_rows(kw_s, SAMPLE_ROWS),
        pad_rows(ki_s, PAGE_SIZE), pad_rows(k_s, PAGE_SIZE), pad_rows(v_s, PAGE_SIZE),
        cache_k[l].reshape(n_pool, PAGE_SIZE, ATTN_WIDTH),
        cache_v[l].reshape(n_pool, PAGE_SIZE, ATTN_WIDTH),
        cache_kidx[l], min(TOPK_MAX, (past + dt) // 4))
    attn_s = attn_s[:, :dt].reshape(ns, ATTN_WIDTH)
    y_sample = _finish(xs, pin_s, attn_s, wpool_bf, pscale_l, wout_bf,
                       ln2_l, w1_bf, w2_bf, lnf_l, ns).reshape(db, dt, d)
    new_k_sample = k_s.reshape(1, db, dt, N_HEADS, HEAD_DIM)
    new_v_sample = v_s.reshape(1, db, dt, N_HEADS, HEAD_DIM)
    new_kidx_sample = ki_s.reshape(1, db, dt, IDX_DIM)
    new_pool_sample = jnp.concatenate([hist, u_s3], axis=1)[:, -POOL_HIST:][None]

    return (y_prompt, y_sample, new_k_prompt, new_v_prompt, new_kidx_prompt, new_pool_prompt,
            new_k_sample, new_v_sample, new_kidx_sample, new_pool_sample)
```

```python
import functools

import jax
import jax.numpy as jnp
from jax import lax
from jax.experimental import pallas as pl
from jax.experimental.pallas import tpu as pltpu

D_MODEL = 1024
POOL_WIDTH = 512
POOL_WINDOWS = (2, 4, 8, 16)
POOL_GROUP = 128
POOL_HIST = 15
ATTN_WIDTH = 512
HEAD_DIM = 64
N_HEADS = 8
IDX_HEADS = 4
IDX_DIM = 64
TOPK_MAX = 256
PAGE_SIZE = 128
D_FF = 4096
NORM_EPS = 1e-6
NEG_INF = -1e30
IN_WIDTH = 2372
IN_PAD = 2432
KW_OFF = 2304

LANES = 128
SUBLANES = 8
ROW_TILE = 512
FINISH_TILE = 256
Q_TILE = LANES
KEY_CHUNK = 512
KEY_SUB = 256
HEAD_PAIRS = N_HEADS // 2
PAGES_PER_STEP = 8
SAMPLE_ROWS = 8
VMEM_LIMIT = 56 * 1024 * 1024

F32 = jnp.float32
BF16 = jnp.bfloat16


def _rmsnorm(x, g):
    return x * lax.rsqrt(jnp.mean(x * x, axis=-1, keepdims=True) + NORM_EPS) * g


def _proj_split(x_ref, ln_ref, w_ref):
    xn = _rmsnorm(x_ref[...], ln_ref[...]).astype(BF16)
    z = jnp.dot(xn, w_ref[...], preferred_element_type=F32)
    return (z[:, 0:512], z[:, 512:1024] * (HEAD_DIM ** -0.5), z[:, 1024:1536], z[:, 1536:2048],
            z[:, 2048:KW_OFF] * (IDX_DIM ** -0.5), z[:, KW_OFF:IN_PAD])


def _proj_rows_kernel(x_ref, ln_ref, w_ref, u_ref, k_ref, v_ref, ki_ref, kw_ref, qb_ref, qib_ref):
    u, q, k, v, qi, kw = _proj_split(x_ref, ln_ref, w_ref)
    u_ref[...] = u
    k_ref[...] = k
    v_ref[...] = v
    ki_ref[...] = kw[:, :IDX_DIM]
    kw_ref[...] = kw
    qb_ref[...] = q.astype(BF16)
    qib_ref[...] = qi.astype(BF16)


def _proj_cols_kernel(x_ref, ln_ref, w_ref, u_ref, kt_ref, vt_ref, kit_ref,
                      kb_ref, kib_ref, qtb_ref, qitb_ref, vtb_ref, kwt_ref):
    u, q, k, v, qi, kw = _proj_split(x_ref, ln_ref, w_ref)
    u_ref[...] = u
    v_t = v.T
    kw_t = kw.T
    kt_ref[0] = k.T
    vt_ref[0] = v_t
    kit_ref[0] = kw_t[:IDX_DIM, :]
    kb_ref[...] = k.astype(BF16)
    kib_ref[...] = kw[:, :IDX_DIM].astype(BF16)
    qtb_ref[0] = q.T.astype(BF16)
    qitb_ref[0] = qi.T.astype(BF16)
    vtb_ref[0] = v_t.astype(BF16)
    kwt_ref[0] = kw_t


def _project(x2d, ln, w_bf, tm, seq_len=None):
    n = x2d.shape[0]
    nt = n // tm
    row = lambda w: pl.BlockSpec((tm, w), lambda i: (i, 0))
    qiw = IDX_HEADS * IDX_DIM
    sds = jax.ShapeDtypeStruct
    if seq_len is None:
        body = _proj_rows_kernel
        out_shape = (sds((n, POOL_WIDTH), F32), sds((n, ATTN_WIDTH), F32), sds((n, ATTN_WIDTH), F32),
                     sds((n, IDX_DIM), F32), sds((n, LANES), F32),
                     sds((n, ATTN_WIDTH), BF16), sds((n, qiw), BF16))
        out_specs = (row(POOL_WIDTH), row(ATTN_WIDTH), row(ATTN_WIDTH), row(IDX_DIM), row(LANES),
                     row(ATTN_WIDTH), row(qiw))
    else:
        body = _proj_cols_kernel
        tps = seq_len // tm
        nb = n // seq_len
        seq = lambda r: pl.BlockSpec((1, r, tm), lambda i: (i // tps, 0, i % tps))
        col = lambda r: pl.BlockSpec((1, r, tm), lambda i: (i, 0, 0))
        out_shape = (sds((n, POOL_WIDTH), F32),
                     sds((nb, ATTN_WIDTH, seq_len), F32),
                     sds((nb, ATTN_WIDTH, seq_len), F32),
                     sds((nb, IDX_DIM, seq_len), F32),
                     sds((n, ATTN_WIDTH), BF16), sds((n, IDX_DIM), BF16),
                     sds((nt, ATTN_WIDTH, tm), BF16),
                     sds((nt, qiw, tm), BF16),
                     sds((nt, ATTN_WIDTH, tm), BF16),
                     sds((nt, LANES, tm), F32))
        out_specs = (row(POOL_WIDTH), seq(ATTN_WIDTH), seq(ATTN_WIDTH), seq(IDX_DIM),
                     row(ATTN_WIDTH), row(IDX_DIM),
                     col(ATTN_WIDTH), col(qiw), col(ATTN_WIDTH), col(LANES))
    return pl.pallas_call(
        body,
        grid=(nt,),
        in_specs=[row(D_MODEL),
                  pl.BlockSpec((1, D_MODEL), lambda i: (0, 0)),
                  pl.BlockSpec((D_MODEL, IN_PAD), lambda i: (0, 0))],
        out_specs=out_specs,
        out_shape=out_shape,
        compiler_params=pltpu.CompilerParams(
            dimension_semantics=("arbitrary",), vmem_limit_bytes=VMEM_LIMIT),
        name="proj",
    )(x2d, ln, w_bf)


def _pool_prompt_kernel(u_ref, halo_ref, o_ref, ext_ref, *, tm, tiles_per_seq):
    it = pl.program_id(0) % tiles_per_seq
    ext_ref[0:16, :] = jnp.where(it == 0, 0.0, halo_ref[...])
    ext_ref[16:, :] = u_ref[...]
    pos = it * tm + lax.broadcasted_iota(jnp.int32, (tm, 1), 0)
    for g, w in enumerate(POOL_WINDOWS):
        cols = slice(g * POOL_GROUP, (g + 1) * POOL_GROUP)
        cur = ext_ref[16:16 + tm, cols]
        s = cur
        for j in range(1, w):
            s = s + ext_ref[16 - j:16 - j + tm, cols]
        cnt = jnp.minimum(pos + 1, w).astype(F32)
        o_ref[:, cols] = (s / cnt - cur).astype(BF16)


def _pool_prompt(u2d, seq_len, tm):
    n = u2d.shape[0]
    hb = tm // 16
    return pl.pallas_call(
        functools.partial(_pool_prompt_kernel, tm=tm, tiles_per_seq=seq_len // tm),
        grid=(n // tm,),
        in_specs=[pl.BlockSpec((tm, POOL_WIDTH), lambda i: (i, 0)),
                  pl.BlockSpec((16, POOL_WIDTH), lambda i: (jnp.maximum(i * hb - 1, 0), 0))],
        out_specs=pl.BlockSpec((tm, POOL_WIDTH), lambda i: (i, 0)),
        out_shape=jax.ShapeDtypeStruct((n, POOL_WIDTH), BF16),
        scratch_shapes=[pltpu.VMEM((tm + 16, POOL_WIDTH), F32)],
        compiler_params=pltpu.CompilerParams(dimension_semantics=("arbitrary",)),
        name="pool_prompt",
    )(u2d, u2d)


def _pool_sample_kernel(u_ref, h_ref, o_ref, *, start_pos):
    t = u_ref.shape[0]

    def row(r, cols):
        return h_ref[POOL_HIST + r, :, cols] if r < 0 else u_ref[r, :, cols]

    for i in range(t):
        for g, w in enumerate(POOL_WINDOWS):
            cols = slice(g * POOL_GROUP, (g + 1) * POOL_GROUP)
            cur = row(i, cols)
            s = cur
            for j in range(1, w):
                s = s + row(i - j, cols)
            cnt = float(min(start_pos + i + 1, w))
            o_ref[i, :, cols] = (s / cnt - cur).astype(BF16)


def _pool_sample(u_t, hist_t, start_pos):
    return pl.pallas_call(
        functools.partial(_pool_sample_kernel, start_pos=start_pos),
        out_shape=jax.ShapeDtypeStruct(u_t.shape, BF16),
        name="pool_sample",
    )(u_t, hist_t)


def _key_to_f32(key):
    bits = key ^ ((key >> 31) & 0x7FFFFFFF)
    return lax.bitcast_convert_type(bits, F32)


def _select_to_bias(s_ref, nch, ch, rows, k, idx_bits):
    lane = lax.broadcasted_iota(jnp.int32, (rows, LANES), 1)
    kf = float(k)

    def count(pred):
        def body(c, acc):
            for u in range(ch):
                j = c * ch + u
                acc = acc + jnp.where(pred(s_ref[j], j), 1.0, 0.0)
            return acc
        acc = lax.fori_loop(0, nch, body, jnp.zeros((rows, LANES), F32))
        return jnp.sum(acc, axis=1, keepdims=True)

    def bcast(key):
        return jnp.broadcast_to(_key_to_f32(key), (rows, LANES))

    def value_step(it, tau):
        cand = tau ^ lax.shift_left(jnp.int32(1), 31 - it)
        cb = bcast(cand)
        cnt = count(lambda blk, j: blk >= cb)
        return jnp.where(cnt >= kf, cand, tau)

    tau = lax.fori_loop(0, 32, value_step, jnp.full((rows, 1), -2 ** 31, jnp.int32))
    tb = bcast(tau)
    need = kf - count(lambda blk, j: blk > tb)
    n_ge = count(lambda blk, j: blk >= tb)

    def index_search():
        def index_step(it, p):
            cand = p | lax.shift_left(jnp.int32(1), idx_bits - 1 - it)
            cb = jnp.broadcast_to(cand, (rows, LANES))
            cnt = count(lambda blk, j: (blk == tb) & ((lane + j * LANES) < cb))
            return jnp.where(cnt < need, cand, p)
        return lax.fori_loop(0, idx_bits, index_step, jnp.zeros((rows, 1), jnp.int32))

    any_tie = jnp.max(jnp.where(n_ge > kf, 1.0, 0.0)) > 0.0
    cut = lax.cond(any_tie, index_search,
                   lambda: jnp.full((rows, 1), 2 ** 30, jnp.int32))
    cutb = jnp.broadcast_to(cut, (rows, LANES))

    def write(c, carry):
        for u in range(ch):
            j = c * ch + u
            blk = s_ref[j]
            keep = (blk > tb) | ((blk == tb) & ((lane + j * LANES) <= cutb))
            s_ref[j] = jnp.where(keep, 0.0, NEG_INF)
        return carry
    lax.fori_loop(0, nch, write, 0)


def _select_to_bias_t(s_ref, nch, k, idx_bits):
    ck = s_ref.shape[1]
    row = lax.broadcasted_iota(jnp.int32, (ck, LANES), 0)
    kf = float(k)

    n_acc = 8
    rows_per_acc = ck // n_acc

    def count(pred):
        def body(c, accs):
            ind = jnp.where(pred(s_ref[c], c), 1.0, 0.0)
            out = []
            for a in range(n_acc):
                part = ind[a * rows_per_acc:(a + 1) * rows_per_acc]
                out.append(accs[a] + jnp.sum(
                    part.reshape(rows_per_acc // SUBLANES, SUBLANES, LANES), axis=0))
            return tuple(out)
        accs = lax.fori_loop(0, nch, body,
                             tuple(jnp.zeros((SUBLANES, LANES), F32) for _ in range(n_acc)))
        return jnp.sum(sum(accs[1:], accs[0]), axis=0, keepdims=True)

    def value_step(it, tau):
        cand = tau ^ lax.shift_left(jnp.int32(1), 31 - it)
        cf = _key_to_f32(cand)
        cnt = count(lambda blk, c: blk >= cf)
        return jnp.where(cnt >= kf, cand, tau)

    tau = lax.fori_loop(0, 32, value_step, jnp.full((1, LANES), -2 ** 31, jnp.int32))
    tf = _key_to_f32(tau)
    need = kf - count(lambda blk, c: blk > tf)
    n_ge = count(lambda blk, c: blk >= tf)

    def index_search():
        def index_step(it, p):
            cand = p | lax.shift_left(jnp.int32(1), idx_bits - 1 - it)
            cnt = count(lambda blk, c: (blk == tf) & ((row + c * ck) < cand))
            return jnp.where(cnt < need, cand, p)
        return lax.fori_loop(0, idx_bits, index_step, jnp.zeros((1, LANES), jnp.int32))

    any_tie = jnp.max(jnp.where(n_ge > kf, 1.0, 0.0)) > 0.0
    cut = lax.cond(any_tie, index_search,
                   lambda: jnp.full((1, LANES), 2 ** 30, jnp.int32))

    def write(c, carry):
        blk = s_ref[c]
        keep = (blk > tf) | ((blk == tf) & ((row + c * ck) <= cut))
        s_ref[c] = jnp.where(keep, 0.0, NEG_INF)
        return carry
    lax.fori_loop(0, nch, write, 0)


_NT = (((1,), (1,)), ((), ()))


def _softmax_step(h, s, v_t, m_ref, l_ref, acc_ref):
    cols = slice(h * HEAD_DIM, (h + 1) * HEAD_DIM)
    m_old = m_ref[h]
    m_new = jnp.maximum(m_old, jnp.max(s, axis=1, keepdims=True))
    alpha = jnp.exp(m_old - m_new)
    p = jnp.exp(s - m_new)
    l_ref[h] = alpha * l_ref[h] + jnp.sum(p, axis=1, keepdims=True)
    acc_ref[:, cols] = alpha * acc_ref[:, cols] + lax.dot_general(
        p.astype(BF16), v_t, _NT, preferred_element_type=F32)
    m_ref[h] = m_new


def _softmax_finish(m_ref, l_ref, acc_ref):
    return jnp.concatenate(
        [acc_ref[:, h * HEAD_DIM:(h + 1) * HEAD_DIM] / l_ref[h] for h in range(N_HEADS)], axis=1)


def _attn_prompt_kernel(qt_ref, qit_ref, kwt_ref, kib_ref, kb_ref, vt_ref, o_ref,
                        s_ref, wq_ref, m_ref, l_ref, acc_ref, *, topk):
    qb = pl.program_id(1)
    nch = qb // (KEY_CHUNK // Q_TILE) + 1
    q_pos = qb * Q_TILE + lax.broadcasted_iota(jnp.int32, (KEY_CHUNK, LANES), 1)
    key_row = lax.broadcasted_iota(jnp.int32, (KEY_CHUNK, LANES), 0)

    qit = qit_ref[0]
    w_rows = [kwt_ref[0, IDX_DIM + h:IDX_DIM + h + 1, :] * (IDX_HEADS ** -0.5)
              for h in range(IDX_HEADS)]
    wqi = [jnp.concatenate([qit[(2 * p) * IDX_DIM:(2 * p + 1) * IDX_DIM, :],
                            qit[(2 * p + 1) * IDX_DIM:(2 * p + 2) * IDX_DIM, :]], axis=1)
           for p in range(IDX_HEADS // 2)]

    def score_chunk(c, carry):
        ki = kib_ref[0, c]
        acc = jnp.zeros((KEY_CHUNK, LANES), F32)
        for p in range(IDX_HEADS // 2):
            r = jnp.dot(ki, wqi[p], preferred_element_type=F32)
            acc = acc + w_rows[2 * p] * jnp.maximum(r[:, :LANES], 0.0)
            acc = acc + w_rows[2 * p + 1] * jnp.maximum(r[:, LANES:], 0.0)
        s_ref[c] = jnp.where(key_row + c * KEY_CHUNK <= q_pos, acc, NEG_INF)
        return carry
    lax.fori_loop(0, nch, score_chunk, 0)

    @pl.when((qb + 1) * Q_TILE <= topk)
    def _():
        def write(c, carry):
            s_ref[c] = jnp.where(key_row + c * KEY_CHUNK <= q_pos, 0.0, NEG_INF)
            return carry
        lax.fori_loop(0, nch, write, 0)

    @pl.when((qb + 1) * Q_TILE > topk)
    def _():
        _select_to_bias_t(s_ref, nch, topk, 13)

    half = lax.broadcasted_iota(jnp.int32, (2 * HEAD_DIM, LANES), 0) < HEAD_DIM
    for p in range(HEAD_PAIRS):
        qp = qt_ref[0, p * 2 * HEAD_DIM:(p + 1) * 2 * HEAD_DIM, :]
        zero = jnp.zeros_like(qp)
        wq_ref[p] = jnp.concatenate([jnp.where(half, qp, zero), jnp.where(half, zero, qp)], axis=1)
    m_ref[...] = jnp.full(m_ref.shape, -jnp.inf, F32)
    l_ref[...] = jnp.zeros(l_ref.shape, F32)
    acc_ref[...] = jnp.zeros(acc_ref.shape, F32)

    blocks = [(sub, p) for sub in range(KEY_CHUNK // KEY_SUB) for p in range(HEAD_PAIRS)]
    keys_of = lambda sub: slice(sub * KEY_SUB, (sub + 1) * KEY_SUB)
    dims_of = lambda p: slice(p * 2 * HEAD_DIM, (p + 1) * 2 * HEAD_DIM)

    def logits(c, sub, p):
        return jnp.dot(kb_ref[0, c, keys_of(sub), dims_of(p)], wq_ref[p],
                       preferred_element_type=F32)

    def attn_chunk(c, carry):
        st_next = logits(c, *blocks[0])
        pending = None
        for i, (sub, p) in enumerate(blocks):
            st = st_next
            if i + 1 < len(blocks):
                st_next = logits(c, *blocks[i + 1])
            bias = s_ref[c, keys_of(sub), :]
            probs, alphas = [], []
            for hh in range(2):
                h = 2 * p + hh
                s = st[:, hh * LANES:(hh + 1) * LANES] + bias
                m_old = m_ref[h]
                m_new = jnp.maximum(m_old, jnp.max(s, axis=0, keepdims=True))
                alpha = jnp.exp(m_old - m_new)
                pr = jnp.exp(s - m_new)
                l_ref[h] = alpha * l_ref[h] + jnp.sum(
                    pr.reshape(KEY_SUB // SUBLANES, SUBLANES, LANES), axis=0)
                m_ref[h] = m_new
                probs.append(pr.astype(BF16))
                alphas.append(alpha)
            pv = jnp.dot(vt_ref[0, c, dims_of(p), keys_of(sub)], jnp.concatenate(probs, axis=1),
                         preferred_element_type=F32)
            if pending is not None:
                pp, pa, ppv = pending
                acc_ref[pp] = acc_ref[pp] * pa + ppv
            pending = (p, jnp.concatenate(alphas, axis=1), pv)
        pp, pa, ppv = pending
        acc_ref[pp] = acc_ref[pp] * pa + ppv
        return carry
    lax.fori_loop(0, nch, attn_chunk, 0)

    outs = []
    for h in range(N_HEADS):
        p, hh = divmod(h, 2)
        o_t = acc_ref[p, hh * HEAD_DIM:(hh + 1) * HEAD_DIM, hh * LANES:(hh + 1) * LANES]
        outs.append(o_t / jnp.sum(l_ref[h], axis=0, keepdims=True))
    o_ref[0] = jnp.concatenate(outs, axis=0).T.astype(BF16)


def _attn_prompt(qt, qit, kwt, kib, kb, vt, b, t, topk):
    nc = t // KEY_CHUNK
    tiles_per_seq = t // ROW_TILE
    qpt = ROW_TILE // Q_TILE
    assert t % KEY_CHUNK == 0 and t <= 2 ** 13 and KEY_CHUNK == ROW_TILE
    tcol = lambda r: pl.BlockSpec((1, r, Q_TILE),
                                  lambda i, j: (i * tiles_per_seq + j // qpt, 0, j % qpt))
    full = lambda r, c: pl.BlockSpec((1, nc, r, c), lambda i, j: (i, 0, 0, 0))
    return pl.pallas_call(
        functools.partial(_attn_prompt_kernel, topk=topk),
        grid=(b, t // Q_TILE),
        in_specs=[tcol(ATTN_WIDTH), tcol(IDX_HEADS * IDX_DIM), tcol(LANES),
                  full(KEY_CHUNK, IDX_DIM), full(KEY_CHUNK, ATTN_WIDTH),
                  full(ATTN_WIDTH, KEY_CHUNK)],
        out_specs=pl.BlockSpec((1, Q_TILE, ATTN_WIDTH), lambda i, j: (i, j, 0)),
        out_shape=jax.ShapeDtypeStruct((b, t, ATTN_WIDTH), BF16),
        scratch_shapes=[pltpu.VMEM((nc, KEY_CHUNK, LANES), F32),
                        pltpu.VMEM((HEAD_PAIRS, 2 * HEAD_DIM, 2 * LANES), BF16),
                        pltpu.VMEM((N_HEADS, 1, LANES), F32),
                        pltpu.VMEM((N_HEADS, SUBLANES, LANES), F32),
                        pltpu.VMEM((HEAD_PAIRS, 2 * HEAD_DIM, 2 * LANES), F32)],
        compiler_params=pltpu.CompilerParams(
            dimension_semantics=("arbitrary", "arbitrary"), vmem_limit_bytes=VMEM_LIMIT),
        name="attn_prompt",
    )(qt, qit, kwt, kib, kb, vt)


def _select_sample_kernel(pt_ref, qi_ref, kw_ref, kin_ref, *rest, n_steps, topk):
    pages = rest[:PAGES_PER_STEP]
    o_ref, s_ref, kbuf_ref = rest[PAGES_PER_STEP:]
    g = pl.program_id(1)
    rows = SAMPLE_ROWS
    qi = qi_ref[0].astype(BF16)
    kw = kw_ref[0]
    w_cols = [kw[:, IDX_DIM + h:IDX_DIM + h + 1] * (IDX_HEADS ** -0.5) for h in range(IDX_HEADS)]

    def scores(keys_t):
        acc = jnp.zeros((rows, keys_t.shape[1]), F32)
        for h in range(IDX_HEADS):
            s = jnp.dot(qi[:, h * IDX_DIM:(h + 1) * IDX_DIM], keys_t, preferred_element_type=F32)
            acc = acc + w_cols[h] * jnp.maximum(s, 0.0)
        return acc

    for j in range(PAGES_PER_STEP):
        kbuf_ref[:, j * PAGE_SIZE:(j + 1) * PAGE_SIZE] = pages[j][0].astype(BF16)
    past = scores(kbuf_ref[...])
    for j in range(PAGES_PER_STEP):
        s_ref[g * PAGES_PER_STEP + j] = past[:, j * PAGE_SIZE:(j + 1) * PAGE_SIZE]

    n_past = n_steps * PAGES_PER_STEP

    @pl.when(g == 0)
    def _():
        new = scores(kin_ref[0].astype(BF16))
        lane = lax.broadcasted_iota(jnp.int32, (rows, LANES), 1)
        row = lax.broadcasted_iota(jnp.int32, (rows, LANES), 0)
        s_ref[n_past] = jnp.where(lane <= row, new, NEG_INF)

    @pl.when(g == n_steps - 1)
    def _():
        _select_to_bias(s_ref, 1, n_past + 1, rows, topk, 14)
        o_ref[0] = s_ref[...]


def _attn_sample_kernel(pt_ref, q_ref, b_ref, kn_ref, vn_ref, *rest):
    kpages = rest[:PAGES_PER_STEP]
    vpages = rest[PAGES_PER_STEP:2 * PAGES_PER_STEP]
    o_ref, kbuf_ref, vbuf_ref, m_ref, l_ref, acc_ref = rest[2 * PAGES_PER_STEP:]
    g = pl.program_id(1)
    n_steps = pl.num_programs(1)
    n_past = b_ref.shape[1] - 1
    q = q_ref[0].astype(BF16)

    def update(k_of, v_of, bias):
        for h in range(N_HEADS):
            cols = slice(h * HEAD_DIM, (h + 1) * HEAD_DIM)
            s = jnp.dot(q[:, cols], k_of(h), preferred_element_type=F32) + bias
            _softmax_step(h, s, v_of(h), m_ref, l_ref, acc_ref)

    @pl.when(g == 0)
    def _():
        m_ref[...] = jnp.full(m_ref.shape, -jnp.inf, F32)
        l_ref[...] = jnp.zeros(l_ref.shape, F32)
        acc_ref[...] = jnp.zeros(acc_ref.shape, F32)
        update(lambda h: kn_ref[0, h].astype(BF16), lambda h: vn_ref[0, h].astype(BF16),
               b_ref[0, n_past])

    for j in range(PAGES_PER_STEP):
        keys = slice(j * PAGE_SIZE, (j + 1) * PAGE_SIZE)
        for h in range(N_HEADS):
            kbuf_ref[h, :, keys] = kpages[j][0, h].astype(BF16)
            vbuf_ref[h, :, keys] = vpages[j][0, h].astype(BF16)
    bias = jnp.concatenate([b_ref[0, g * PAGES_PER_STEP + j] for j in range(PAGES_PER_STEP)], axis=1)
    update(lambda h: kbuf_ref[h], lambda h: vbuf_ref[h], bias)

    @pl.when(g == n_steps - 1)
    def _():
        o_ref[0] = _softmax_finish(m_ref, l_ref, acc_ref)


def _page_specs(block):
    zeros = (0,) * (len(block) - 1)

    def spec(j):
        return pl.BlockSpec(block, lambda b, g, pt: (pt[b, g * PAGES_PER_STEP + j],) + zeros)
    return [spec(j) for j in range(PAGES_PER_STEP)]


def _attn_sample(page_table, q_s, qi_s, kw_s, kin_t, kn_t, vn_t, k_pages, v_pages, ki_pages, topk):
    db, n_pages = page_table.shape
    assert n_pages % PAGES_PER_STEP == 0
    n_steps = n_pages // PAGES_PER_STEP
    nblk = n_pages + 1
    keys_per_step = PAGES_PER_STEP * PAGE_SIZE
    per_seq = lambda *blk: pl.BlockSpec((1,) + blk, lambda b, g, pt: (b,) + (0,) * len(blk))
    bias = pl.pallas_call(
        functools.partial(_select_sample_kernel, n_steps=n_steps, topk=topk),
        grid_spec=pltpu.PrefetchScalarGridSpec(
            num_scalar_prefetch=1,
            grid=(db, n_steps),
            in_specs=[per_seq(SAMPLE_ROWS, IDX_HEADS * IDX_DIM), per_seq(SAMPLE_ROWS, LANES),
                      per_seq(IDX_DIM, PAGE_SIZE)] + _page_specs((1, IDX_DIM, PAGE_SIZE)),
            out_specs=per_seq(nblk, SAMPLE_ROWS, LANES),
            scratch_shapes=[pltpu.VMEM((nblk, SAMPLE_ROWS, LANES), F32),
                            pltpu.VMEM((IDX_DIM, keys_per_step), BF16)]),
        out_shape=jax.ShapeDtypeStruct((db, nblk, SAMPLE_ROWS, LANES), F32),
        compiler_params=pltpu.CompilerParams(dimension_semantics=("arbitrary", "arbitrary")),
        name="select_sample",
    )(page_table, qi_s, kw_s, kin_t, *([ki_pages] * PAGES_PER_STEP))

    head_page = (1, N_HEADS, HEAD_DIM, PAGE_SIZE)
    return pl.pallas_call(
        _attn_sample_kernel,
        grid_spec=pltpu.PrefetchScalarGridSpec(
            num_scalar_prefetch=1,
            grid=(db, n_steps),
            in_specs=[per_seq(SAMPLE_ROWS, ATTN_WIDTH), per_seq(nblk, SAMPLE_ROWS, LANES),
                      per_seq(N_HEADS, HEAD_DIM, PAGE_SIZE), per_seq(N_HEADS, HEAD_DIM, PAGE_SIZE)]
            + _page_specs(head_page) + _page_specs(head_page),
            out_specs=per_seq(SAMPLE_ROWS, ATTN_WIDTH),
            scratch_shapes=[pltpu.VMEM((N_HEADS, HEAD_DIM, keys_per_step), BF16),
                            pltpu.VMEM((N_HEADS, HEAD_DIM, keys_per_step), BF16),
                            pltpu.VMEM((N_HEADS, SAMPLE_ROWS, 1), F32),
                            pltpu.VMEM((N_HEADS, SAMPLE_ROWS, 1), F32),
                            pltpu.VMEM((SAMPLE_ROWS, ATTN_WIDTH), F32)]),
        out_shape=jax.ShapeDtypeStruct((db, SAMPLE_ROWS, ATTN_WIDTH), F32),
        compiler_params=pltpu.CompilerParams(
            dimension_semantics=("arbitrary", "arbitrary"), vmem_limit_bytes=VMEM_LIMIT),
        name="attn_sample",
    )(page_table, q_s, bias, kn_t, vn_t,
      *([k_pages] * PAGES_PER_STEP), *([v_pages] * PAGES_PER_STEP))


def _finish_kernel(x_ref, pin_ref, attn_ref, wpool_ref, pscale_ref, wout_ref, ln2_ref,
                   w1_ref, w2_ref, lnf_ref, o_ref):
    pool = jnp.concatenate(
        [jnp.dot(pin_ref[:, g * POOL_GROUP:(g + 1) * POOL_GROUP], wpool_ref[g],
                 preferred_element_type=F32) for g in range(len(POOL_WINDOWS))], axis=1)
    pool = (pool * pscale_ref[...]).astype(BF16)
    mixed = jnp.concatenate([pool, attn_ref[...].astype(BF16)], axis=1)
    h = x_ref[...] + jnp.dot(mixed, wout_ref[...], preferred_element_type=F32)
    hn = _rmsnorm(h, ln2_ref[...]).astype(BF16)
    y = h
    fc = 1024
    for c in range(D_FF // fc):
        a = jnp.dot(hn, w1_ref[:, c * fc:(c + 1) * fc], preferred_element_type=F32)
        a = jnp.square(jnp.maximum(a, 0.0)).astype(BF16)
        y = y + jnp.dot(a, w2_ref[c * fc:(c + 1) * fc, :], preferred_element_type=F32)
    o_ref[...] = _rmsnorm(y, lnf_ref[...])


def _finish(x2d, pin, attn, wpool, pscale, wout, ln2, w1, w2, lnf, tm):
    n = x2d.shape[0]
    row = lambda w: pl.BlockSpec((tm, w), lambda i: (i, 0))
    const = lambda shape: pl.BlockSpec(shape, lambda i: (0,) * len(shape))
    return pl.pallas_call(
        _finish_kernel,
        grid=(n // tm,),
        in_specs=[row(D_MODEL), row(POOL_WIDTH), row(ATTN_WIDTH),
                  const(wpool.shape), const(pscale.shape), const(wout.shape), const(ln2.shape),
                  const(w1.shape), const(w2.shape), const(lnf.shape)],
        out_specs=row(D_MODEL),
        out_shape=jax.ShapeDtypeStruct((n, D_MODEL), F32),
        compiler_params=pltpu.CompilerParams(
            dimension_semantics=("arbitrary",), vmem_limit_bytes=VMEM_LIMIT),
        name="finish",
    )(x2d, pin, attn, wpool, pscale, wout, ln2, w1, w2, lnf)


def kernel(x_prompt, x_sample, cache_k, cache_v, cache_kidx, state_pool, page_table,
           ln1, w_in, w_pool, pool_scale, w_out, ln2, w_ff1, w_ff2, ln_f):
    depth = ln1.shape[0]
    assert depth == 1
    b, t, d = x_prompt.shape
    db, dt, _ = x_sample.shape
    n_pool = cache_k.shape[1]
    past = page_table.shape[1] * PAGE_SIZE
    l = 0

    w_in_bf = jnp.pad(w_in[l], ((0, 0), (0, IN_PAD - IN_WIDTH))).astype(BF16)
    wpool_bf = w_pool[l].astype(BF16)
    wout_bf = w_out[l].astype(BF16)
    w1_bf = w_ff1[l].astype(BF16)
    w2_bf = w_ff2[l].astype(BF16)
    ln1_l = ln1[l].reshape(1, d)
    ln2_l = ln2[l].reshape(1, d)
    lnf_l = ln_f.reshape(1, d)
    pscale_l = pool_scale[l].reshape(1, POOL_WIDTH)

    xp = x_prompt.reshape(b * t, d)
    u, k_t, v_t, ki_t, kb, kib, qt, qit, vtb, kwt = _project(xp, ln1_l, w_in_bf, ROW_TILE, seq_len=t)
    pin = _pool_prompt(u, t, ROW_TILE)
    nc = t // KEY_CHUNK
    attn = _attn_prompt(
        qt, qit, kwt, kib.reshape(b, nc, KEY_CHUNK, IDX_DIM),
        kb.reshape(b, nc, KEY_CHUNK, ATTN_WIDTH), vtb.reshape(b, nc, ATTN_WIDTH, KEY_CHUNK),
        b, t, min(TOPK_MAX, t // 4))
    y_prompt = _finish(xp, pin, attn.reshape(b * t, ATTN_WIDTH), wpool_bf, pscale_l, wout_bf,
                       ln2_l, w1_bf, w2_bf, lnf_l, FINISH_TILE).reshape(b, t, d)
    new_k_prompt = jnp.transpose(k_t.reshape(b, N_HEADS, HEAD_DIM, t), (0, 3, 1, 2))[None]
    new_v_prompt = jnp.transpose(v_t.reshape(b, N_HEADS, HEAD_DIM, t), (0, 3, 1, 2))[None]
    new_kidx_prompt = jnp.transpose(ki_t, (0, 2, 1))[None]
    new_pool_prompt = u.reshape(b, t, POOL_WIDTH)[:, t - POOL_HIST:][None]

    ns = db * dt
    xs = x_sample.reshape(ns, d)
    u_s, k_s, v_s, ki_s, kw_s, qb_s, qib_s = _project(xs, ln1_l, w_in_bf, ns)
    hist = state_pool[l]
    u_s3 = u_s.reshape(db, dt, POOL_WIDTH)
    pin_s = _pool_sample(jnp.swapaxes(u_s3, 0, 1), jnp.swapaxes(hist, 0, 1), past)
    pin_s = jnp.swapaxes(pin_s, 0, 1).reshape(ns, POOL_WIDTH)

    def pad_rows(a, rows):
        a = a.reshape(db, dt, a.shape[-1]).astype(F32)
        return jnp.pad(a, ((0, 0), (0, rows - dt), (0, 0)))

    def tokens_last(a, *mid):
        a = jnp.moveaxis(a.reshape((db, dt) + mid), 1, -1)
        return jnp.pad(a, ((0, 0),) * (a.ndim - 1) + ((0, PAGE_SIZE - dt),))

    attn_s = _attn_sample(
        page_table,
        pad_rows(qb_s, SAMPLE_ROWS),
        pad_rows(qib_s, SAMPLE_ROWS),
        pad_rows(kw_s, SAMPLE_ROWS),
        tokens_last(ki_s, IDX_DIM), tokens_last(k_s, N_HEADS, HEAD_DIM),
        tokens_last(v_s, N_HEADS, HEAD_DIM),
        jnp.transpose(cache_k[l], (0, 2, 3, 1)), jnp.transpose(cache_v[l], (0, 2, 3, 1)),
        jnp.transpose(cache_kidx[l], (0, 2, 1)), min(TOPK_MAX, (past + dt) // 4))
    attn_s = attn_s[:, :dt].reshape(ns, ATTN_WIDTH)
    y_sample = _finish(xs, pin_s, attn_s, wpool_bf, pscale_l, wout_bf,
                       ln2_l, w1_bf, w2_bf, lnf_l, ns).reshape(db, dt, d)
    new_k_sample = k_s.reshape(1, db, dt, N_HEADS, HEAD_DIM)
    new_v_sample = v_s.reshape(1, db, dt, N_HEADS, HEAD_DIM)
    new_kidx_sample = ki_s.reshape(1, db, dt, IDX_DIM)
    new_pool_sample = jnp.concatenate([hist, u_s3], axis=1)[:, -POOL_HIST:][None]

    return (y_prompt, y_sample, new_k_prompt, new_v_prompt, new_kidx_prompt, new_pool_prompt,
            new_k_sample, new_v_sample, new_kidx_sample, new_pool_sample)
```

```python
import functools

import jax
import jax.numpy as jnp
from jax import lax
from jax.experimental import pallas as pl
from jax.experimental.pallas import tpu as pltpu

D_MODEL = 1024
POOL_WIDTH = 512
POOL_WINDOWS = (2, 4, 8, 16)
POOL_GROUP = 128
POOL_HIST = 15
ATTN_WIDTH = 512
HEAD_DIM = 64
N_HEADS = 8
IDX_HEADS = 4
IDX_DIM = 64
TOPK_MAX = 256
PAGE_SIZE = 128
D_FF = 4096
NORM_EPS = 1e-6
NEG_INF = -1e30
IN_WIDTH = 2372
IN_PAD = 2432
KW_OFF = 2304

LANES = 128
SUBLANES = 8
ROW_TILE = 512
FINISH_TILE = 256
Q_TILE = LANES
KEY_CHUNK = 512
KEY_SUB = 256
HEAD_PAIRS = N_HEADS // 2
V_EXT = 2 * HEAD_DIM + 16
LOG2_E = 1.4426950408889634
PAGES_PER_STEP = 8
SAMPLE_ROWS = 8
VMEM_LIMIT = 56 * 1024 * 1024

F32 = jnp.float32
BF16 = jnp.bfloat16


def _rmsnorm(x, g):
    return x * lax.rsqrt(jnp.mean(x * x, axis=-1, keepdims=True) + NORM_EPS) * g


def _proj_split(x_ref, ln_ref, w_ref):
    xn = _rmsnorm(x_ref[...], ln_ref[...]).astype(BF16)
    z = jnp.dot(xn, w_ref[...], preferred_element_type=F32)
    return (z[:, 0:512], z[:, 512:1024] * (HEAD_DIM ** -0.5), z[:, 1024:1536], z[:, 1536:2048],
            z[:, 2048:KW_OFF] * (IDX_DIM ** -0.5), z[:, KW_OFF:IN_PAD])


def _proj_rows_kernel(x_ref, ln_ref, w_ref, u_ref, k_ref, v_ref, ki_ref, kw_ref, qb_ref, qib_ref):
    u, q, k, v, qi, kw = _proj_split(x_ref, ln_ref, w_ref)
    u_ref[...] = u
    k_ref[...] = k
    v_ref[...] = v
    ki_ref[...] = kw[:, :IDX_DIM]
    kw_ref[...] = kw
    qb_ref[...] = q.astype(BF16)
    qib_ref[...] = qi.astype(BF16)


def _proj_cols_kernel(x_ref, ln_ref, w_ref, u_ref, kt_ref, vt_ref, kit_ref,
                      kb_ref, kib_ref, qtb_ref, qitb_ref, vtb_ref, kwt_ref):
    u, q, k, v, qi, kw = _proj_split(x_ref, ln_ref, w_ref)
    u_ref[...] = u
    v_t = v.T
    kw_t = kw.T
    kt_ref[0] = k.T
    vt_ref[0] = v_t
    kit_ref[0] = kw_t[:IDX_DIM, :]
    kb_ref[...] = k.astype(BF16)
    kib_ref[...] = kw[:, :IDX_DIM].astype(BF16)
    qtb_ref[0] = (q * LOG2_E).T.astype(BF16)
    qitb_ref[0] = qi.T.astype(BF16)
    ones = jnp.ones((V_EXT - 2 * HEAD_DIM, v_t.shape[1]), F32)
    pieces = []
    for p in range(HEAD_PAIRS):
        pieces += [v_t[p * 2 * HEAD_DIM:(p + 1) * 2 * HEAD_DIM], ones]
    vtb_ref[0] = jnp.concatenate(pieces, axis=0).astype(BF16)
    kwt_ref[0] = kw_t


def _project(x2d, ln, w_bf, tm, seq_len=None):
    n = x2d.shape[0]
    nt = n // tm
    row = lambda w: pl.BlockSpec((tm, w), lambda i: (i, 0))
    qiw = IDX_HEADS * IDX_DIM
    sds = jax.ShapeDtypeStruct
    if seq_len is None:
        body = _proj_rows_kernel
        out_shape = (sds((n, POOL_WIDTH), F32), sds((n, ATTN_WIDTH), F32), sds((n, ATTN_WIDTH), F32),
                     sds((n, IDX_DIM), F32), sds((n, LANES), F32),
                     sds((n, ATTN_WIDTH), BF16), sds((n, qiw), BF16))
        out_specs = (row(POOL_WIDTH), row(ATTN_WIDTH), row(ATTN_WIDTH), row(IDX_DIM), row(LANES),
                     row(ATTN_WIDTH), row(qiw))
    else:
        body = _proj_cols_kernel
        tps = seq_len // tm
        nb = n // seq_len
        seq = lambda r: pl.BlockSpec((1, r, tm), lambda i: (i // tps, 0, i % tps))
        col = lambda r: pl.BlockSpec((1, r, tm), lambda i: (i, 0, 0))
        out_shape = (sds((n, POOL_WIDTH), F32),
                     sds((nb, ATTN_WIDTH, seq_len), F32),
                     sds((nb, ATTN_WIDTH, seq_len), F32),
                     sds((nb, IDX_DIM, seq_len), F32),
                     sds((n, ATTN_WIDTH), BF16), sds((n, IDX_DIM), BF16),
                     sds((nt, ATTN_WIDTH, tm), BF16),
                     sds((nt, qiw, tm), BF16),
                     sds((nt, HEAD_PAIRS * V_EXT, tm), BF16),
                     sds((nt, LANES, tm), F32))
        out_specs = (row(POOL_WIDTH), seq(ATTN_WIDTH), seq(ATTN_WIDTH), seq(IDX_DIM),
                     row(ATTN_WIDTH), row(IDX_DIM),
                     col(ATTN_WIDTH), col(qiw), col(HEAD_PAIRS * V_EXT), col(LANES))
    return pl.pallas_call(
        body,
        grid=(nt,),
        in_specs=[row(D_MODEL),
                  pl.BlockSpec((1, D_MODEL), lambda i: (0, 0)),
                  pl.BlockSpec((D_MODEL, IN_PAD), lambda i: (0, 0))],
        out_specs=out_specs,
        out_shape=out_shape,
        compiler_params=pltpu.CompilerParams(
            dimension_semantics=("arbitrary",), vmem_limit_bytes=VMEM_LIMIT),
        name="proj",
    )(x2d, ln, w_bf)


def _pool_prompt_kernel(u_ref, halo_ref, o_ref, ext_ref, *, tm, tiles_per_seq):
    it = pl.program_id(0) % tiles_per_seq
    ext_ref[0:16, :] = jnp.where(it == 0, 0.0, halo_ref[...])
    ext_ref[16:, :] = u_ref[...]
    pos = it * tm + lax.broadcasted_iota(jnp.int32, (tm, 1), 0)
    for g, w in enumerate(POOL_WINDOWS):
        cols = slice(g * POOL_GROUP, (g + 1) * POOL_GROUP)
        cur = ext_ref[16:16 + tm, cols]
        s = cur
        for j in range(1, w):
            s = s + ext_ref[16 - j:16 - j + tm, cols]
        cnt = jnp.minimum(pos + 1, w).astype(F32)
        o_ref[:, cols] = (s / cnt - cur).astype(BF16)


def _pool_prompt(u2d, seq_len, tm):
    n = u2d.shape[0]
    hb = tm // 16
    return pl.pallas_call(
        functools.partial(_pool_prompt_kernel, tm=tm, tiles_per_seq=seq_len // tm),
        grid=(n // tm,),
        in_specs=[pl.BlockSpec((tm, POOL_WIDTH), lambda i: (i, 0)),
                  pl.BlockSpec((16, POOL_WIDTH), lambda i: (jnp.maximum(i * hb - 1, 0), 0))],
        out_specs=pl.BlockSpec((tm, POOL_WIDTH), lambda i: (i, 0)),
        out_shape=jax.ShapeDtypeStruct((n, POOL_WIDTH), BF16),
        scratch_shapes=[pltpu.VMEM((tm + 16, POOL_WIDTH), F32)],
        compiler_params=pltpu.CompilerParams(dimension_semantics=("arbitrary",)),
        name="pool_prompt",
    )(u2d, u2d)


def _pool_sample_kernel(u_ref, h_ref, o_ref, *, start_pos):
    t = u_ref.shape[0]

    def row(r, cols):
        return h_ref[POOL_HIST + r, :, cols] if r < 0 else u_ref[r, :, cols]

    for i in range(t):
        for g, w in enumerate(POOL_WINDOWS):
            cols = slice(g * POOL_GROUP, (g + 1) * POOL_GROUP)
            cur = row(i, cols)
            s = cur
            for j in range(1, w):
                s = s + row(i - j, cols)
            cnt = float(min(start_pos + i + 1, w))
            o_ref[i, :, cols] = (s / cnt - cur).astype(BF16)


def _pool_sample(u_t, hist_t, start_pos):
    return pl.pallas_call(
        functools.partial(_pool_sample_kernel, start_pos=start_pos),
        out_shape=jax.ShapeDtypeStruct(u_t.shape, BF16),
        name="pool_sample",
    )(u_t, hist_t)


def _key_to_f32(key):
    bits = key ^ ((key >> 31) & 0x7FFFFFFF)
    return lax.bitcast_convert_type(bits, F32)


def _kth_largest_key(count_ge, shape, kf):
    def bit_step(it, st):
        tau, n_tau = st
        cand = tau ^ lax.shift_left(jnp.int32(1), 31 - it)
        cnt = count_ge(cand)
        ok = cnt >= kf
        return jnp.where(ok, cand, tau), jnp.where(ok, cnt, n_tau)

    return lax.fori_loop(0, 32, bit_step,
                         (jnp.full(shape, -2 ** 31, jnp.int32), jnp.full(shape, jnp.inf, F32)))


def _select_to_bias(s_ref, nch, ch, rows, real_rows, k, idx_bits):
    lane = lax.broadcasted_iota(jnp.int32, (rows, LANES), 1)
    kf = float(k)
    n_acc = min(4, ch)

    def count(pred):
        def body(c, accs):
            accs = list(accs)
            for u in range(ch):
                j = c * ch + u
                accs[u % n_acc] = accs[u % n_acc] + jnp.where(pred(s_ref[j], j), 1.0, 0.0)
            return tuple(accs)
        accs = lax.fori_loop(0, nch, body,
                             tuple(jnp.zeros((rows, LANES), F32) for _ in range(n_acc)))
        return jnp.sum(sum(accs[1:], accs[0]), axis=1, keepdims=True)

    def bcast(key):
        return jnp.broadcast_to(_key_to_f32(key), (rows, LANES))

    def count_ge(key):
        cb = bcast(key)
        return count(lambda blk, j: blk >= cb)

    tau, n_ge = _kth_largest_key(count_ge, (rows, 1), kf)
    tb = bcast(tau)
    need = kf - count(lambda blk, j: blk > tb)

    def index_search():
        def index_step(it, p):
            cand = p | lax.shift_left(jnp.int32(1), idx_bits - 1 - it)
            cb = jnp.broadcast_to(cand, (rows, LANES))
            cnt = count(lambda blk, j: (blk == tb) & ((lane + j * LANES) < cb))
            return jnp.where(cnt < need, cand, p)
        return lax.fori_loop(0, idx_bits, index_step, jnp.zeros((rows, 1), jnp.int32))

    real = lax.broadcasted_iota(jnp.int32, (rows, 1), 0) < real_rows
    any_tie = jnp.max(jnp.where((n_ge > kf) & real, 1.0, 0.0)) > 0.0
    cut = lax.cond(any_tie, index_search,
                   lambda: jnp.full((rows, 1), 2 ** 30, jnp.int32))
    cutb = jnp.broadcast_to(cut, (rows, LANES))

    def write(c, carry):
        for u in range(ch):
            j = c * ch + u
            blk = s_ref[j]
            keep = (blk > tb) | ((blk == tb) & ((lane + j * LANES) <= cutb))
            s_ref[j] = jnp.where(keep, 0.0, NEG_INF)
        return carry
    lax.fori_loop(0, nch, write, 0)


def _select_to_bias_t(s_ref, tri_ref, nch, k):
    ck = s_ref.shape[1]
    kf = float(k)

    n_acc = 8
    rows_per_acc = ck // n_acc

    def count(pred):
        def body(c, accs):
            ind = jnp.where(pred(s_ref[c]), 1.0, 0.0)
            out = []
            for a in range(n_acc):
                part = ind[a * rows_per_acc:(a + 1) * rows_per_acc]
                out.append(accs[a] + jnp.sum(
                    part.reshape(rows_per_acc // SUBLANES, SUBLANES, LANES), axis=0))
            return tuple(out)
        accs = lax.fori_loop(0, nch, body,
                             tuple(jnp.zeros((SUBLANES, LANES), F32) for _ in range(n_acc)))
        return jnp.sum(sum(accs[1:], accs[0]), axis=0, keepdims=True)

    def count_ge(key):
        cf = _key_to_f32(key)
        return count(lambda blk: blk >= cf)

    tau, _ = _kth_largest_key(count_ge, (1, LANES), kf)
    tf = _key_to_f32(tau)
    need = kf - count(lambda blk: blk > tf)

    def write(c, seen):
        blk = s_ref[c]
        tied = blk == tf
        rank = seen + jnp.dot(tri_ref[...], jnp.where(tied, 1.0, 0.0).astype(BF16),
                              preferred_element_type=F32)
        keep = (blk > tf) | (tied & (rank <= need))
        s_ref[c] = jnp.where(keep, 0.0, NEG_INF)
        return rank[ck - 1:ck, :]
    lax.fori_loop(0, nch, write, jnp.zeros((1, LANES), F32))


_NT = (((1,), (1,)), ((), ()))


def _softmax_finish(m_ref, l_ref, acc_ref):
    return jnp.concatenate(
        [acc_ref[:, h * HEAD_DIM:(h + 1) * HEAD_DIM] / l_ref[h] for h in range(N_HEADS)], axis=1)


def _attn_prompt_kernel(qt_ref, qit_ref, kwt_ref, kib_ref, kb_ref, vt_ref, tri_ref, o_ref,
                        s_ref, wq_ref, m_ref, acc_ref, *, topk):
    qb = pl.program_id(1)
    nch = qb // (KEY_CHUNK // Q_TILE) + 1
    q_pos = qb * Q_TILE + lax.broadcasted_iota(jnp.int32, (KEY_CHUNK, LANES), 1)
    key_row = lax.broadcasted_iota(jnp.int32, (KEY_CHUNK, LANES), 0)

    qit = qit_ref[0]
    w_rows = [kwt_ref[0, IDX_DIM + h:IDX_DIM + h + 1, :] * (IDX_HEADS ** -0.5)
              for h in range(IDX_HEADS)]
    wqi = [jnp.concatenate([qit[(2 * p) * IDX_DIM:(2 * p + 1) * IDX_DIM, :],
                            qit[(2 * p + 1) * IDX_DIM:(2 * p + 2) * IDX_DIM, :]], axis=1)
           for p in range(IDX_HEADS // 2)]

    def score_chunk(c, carry):
        ki = kib_ref[0, c]
        acc = jnp.zeros((KEY_CHUNK, LANES), F32)
        for p in range(IDX_HEADS // 2):
            r = jnp.dot(ki, wqi[p], preferred_element_type=F32)
            acc = acc + w_rows[2 * p] * jnp.maximum(r[:, :LANES], 0.0)
            acc = acc + w_rows[2 * p + 1] * jnp.maximum(r[:, LANES:], 0.0)
        s_ref[c] = jnp.where(key_row + c * KEY_CHUNK <= q_pos, acc, NEG_INF)
        return carry
    lax.fori_loop(0, nch, score_chunk, 0)

    @pl.when((qb + 1) * Q_TILE <= topk)
    def _():
        def write(c, carry):
            s_ref[c] = jnp.where(key_row + c * KEY_CHUNK <= q_pos, 0.0, NEG_INF)
            return carry
        lax.fori_loop(0, nch, write, 0)

    @pl.when((qb + 1) * Q_TILE > topk)
    def _():
        _select_to_bias_t(s_ref, tri_ref, nch, topk)

    half = lax.broadcasted_iota(jnp.int32, (2 * HEAD_DIM, LANES), 0) < HEAD_DIM
    for p in range(HEAD_PAIRS):
        qp = qt_ref[0, p * 2 * HEAD_DIM:(p + 1) * 2 * HEAD_DIM, :]
        zero = jnp.zeros_like(qp)
        wq_ref[p] = jnp.concatenate([jnp.where(half, qp, zero), jnp.where(half, zero, qp)], axis=1)
    m_ref[...] = jnp.full(m_ref.shape, -jnp.inf, F32)
    acc_ref[...] = jnp.zeros(acc_ref.shape, F32)

    blocks = [(sub, p) for sub in range(KEY_CHUNK // KEY_SUB) for p in range(HEAD_PAIRS)]
    keys_of = lambda sub: slice(sub * KEY_SUB, (sub + 1) * KEY_SUB)
    dims_of = lambda p: slice(p * 2 * HEAD_DIM, (p + 1) * 2 * HEAD_DIM)
    vrows_of = lambda p: slice(p * V_EXT, (p + 1) * V_EXT)

    def logits(c, sub, p):
        return jnp.dot(kb_ref[0, c, keys_of(sub), dims_of(p)], wq_ref[p],
                       preferred_element_type=F32)

    def attn_chunk(c, carry):
        st_next = logits(c, *blocks[0])
        pending = None
        for i, (sub, p) in enumerate(blocks):
            st = st_next
            if i + 1 < len(blocks):
                st_next = logits(c, *blocks[i + 1])
            bias = s_ref[c, keys_of(sub), :]
            probs, alphas = [], []
            for hh in range(2):
                h = 2 * p + hh
                s = st[:, hh * LANES:(hh + 1) * LANES] + bias
                m_old = m_ref[h]
                m_new = jnp.maximum(m_old, jnp.max(s, axis=0, keepdims=True))
                alphas.append(jnp.exp2(m_old - m_new))
                probs.append(jnp.exp2(s - m_new).astype(BF16))
                m_ref[h] = m_new
            pv = jnp.dot(vt_ref[0, c, vrows_of(p), keys_of(sub)], jnp.concatenate(probs, axis=1),
                         preferred_element_type=F32)
            if pending is not None:
                pp, pa, ppv = pending
                acc_ref[pp] = acc_ref[pp] * pa + ppv
            pending = (p, jnp.concatenate(alphas, axis=1), pv)
        pp, pa, ppv = pending
        acc_ref[pp] = acc_ref[pp] * pa + ppv
        return carry
    lax.fori_loop(0, nch, attn_chunk, 0)

    outs = []
    for h in range(N_HEADS):
        p, hh = divmod(h, 2)
        lanes = slice(hh * LANES, (hh + 1) * LANES)
        o_t = acc_ref[p, hh * HEAD_DIM:(hh + 1) * HEAD_DIM, lanes]
        outs.append(o_t / acc_ref[p, 2 * HEAD_DIM:2 * HEAD_DIM + 1, lanes])
    o_ref[0] = jnp.concatenate(outs, axis=0).T.astype(BF16)


def _attn_prompt(qt, qit, kwt, kib, kb, vt, b, t, topk):
    nc = t // KEY_CHUNK
    tiles_per_seq = t // ROW_TILE
    qpt = ROW_TILE // Q_TILE
    assert t % KEY_CHUNK == 0 and t <= 2 ** 13 and KEY_CHUNK == ROW_TILE
    tcol = lambda r: pl.BlockSpec((1, r, Q_TILE),
                                  lambda i, j: (i * tiles_per_seq + j // qpt, 0, j % qpt))
    full = lambda r, c: pl.BlockSpec((1, nc, r, c), lambda i, j: (i, 0, 0, 0))
    return pl.pallas_call(
        functools.partial(_attn_prompt_kernel, topk=topk),
        grid=(b, t // Q_TILE),
        in_specs=[tcol(ATTN_WIDTH), tcol(IDX_HEADS * IDX_DIM), tcol(LANES),
                  full(KEY_CHUNK, IDX_DIM), full(KEY_CHUNK, ATTN_WIDTH),
                  full(HEAD_PAIRS * V_EXT, KEY_CHUNK),
                  pl.BlockSpec((KEY_CHUNK, KEY_CHUNK), lambda i, j: (0, 0))],
        out_specs=pl.BlockSpec((1, Q_TILE, ATTN_WIDTH), lambda i, j: (i, j, 0)),
        out_shape=jax.ShapeDtypeStruct((b, t, ATTN_WIDTH), BF16),
        scratch_shapes=[pltpu.VMEM((nc, KEY_CHUNK, LANES), F32),
                        pltpu.VMEM((HEAD_PAIRS, 2 * HEAD_DIM, 2 * LANES), BF16),
                        pltpu.VMEM((N_HEADS, 1, LANES), F32),
                        pltpu.VMEM((HEAD_PAIRS, V_EXT, 2 * LANES), F32)],
        compiler_params=pltpu.CompilerParams(
            dimension_semantics=("arbitrary", "arbitrary"), vmem_limit_bytes=VMEM_LIMIT),
        name="attn_prompt",
    )(qt, qit, kwt, kib, kb, vt, jnp.tril(jnp.ones((KEY_CHUNK, KEY_CHUNK), BF16)))


def _select_sample_kernel(pt_ref, qi_ref, kw_ref, kin_ref, *rest, n_steps, real_rows, topk):
    pages = rest[:PAGES_PER_STEP]
    o_ref, s_ref, kbuf_ref = rest[PAGES_PER_STEP:]
    g = pl.program_id(1)
    rows = SAMPLE_ROWS
    qi = qi_ref[0].astype(BF16)
    kw = kw_ref[0]
    w_cols = [kw[:, IDX_DIM + h:IDX_DIM + h + 1] * (IDX_HEADS ** -0.5) for h in range(IDX_HEADS)]

    def scores(keys_t):
        acc = jnp.zeros((rows, keys_t.shape[1]), F32)
        for h in range(IDX_HEADS):
            s = jnp.dot(qi[:, h * IDX_DIM:(h + 1) * IDX_DIM], keys_t, preferred_element_type=F32)
            acc = acc + w_cols[h] * jnp.maximum(s, 0.0)
        return acc

    for j in range(PAGES_PER_STEP):
        kbuf_ref[:, j * PAGE_SIZE:(j + 1) * PAGE_SIZE] = pages[j][0].astype(BF16)
    past = scores(kbuf_ref[...])
    for j in range(PAGES_PER_STEP):
        s_ref[g * PAGES_PER_STEP + j] = past[:, j * PAGE_SIZE:(j + 1) * PAGE_SIZE]

    n_past = n_steps * PAGES_PER_STEP

    @pl.when(g == 0)
    def _():
        new = scores(kin_ref[0].astype(BF16))
        lane = lax.broadcasted_iota(jnp.int32, (rows, LANES), 1)
        row = lax.broadcasted_iota(jnp.int32, (rows, LANES), 0)
        s_ref[n_past] = jnp.where(lane <= row, new, NEG_INF)

    @pl.when(g == n_steps - 1)
    def _():
        _select_to_bias(s_ref, 1, n_past + 1, rows, real_rows, topk, 14)
        o_ref[0] = s_ref[...]


def _attn_sample_kernel(pt_ref, q_ref, b_ref, kn_ref, vn_ref, *rest):
    kpages = rest[:PAGES_PER_STEP]
    vpages = rest[PAGES_PER_STEP:2 * PAGES_PER_STEP]
    o_ref, kbuf_ref, vbuf_ref, m_ref, l_ref, acc_ref = rest[2 * PAGES_PER_STEP:]
    g = pl.program_id(1)
    n_steps = pl.num_programs(1)
    n_past = b_ref.shape[1] - 1
    q = q_ref[0].astype(BF16)

    def update(k_of, v_of, bias):
        cols = [slice(h * HEAD_DIM, (h + 1) * HEAD_DIM) for h in range(N_HEADS)]
        logits = [jnp.dot(q[:, cols[h]], k_of(h), preferred_element_type=F32) + bias
                  for h in range(N_HEADS)]
        probs, alphas = [], []
        for h in range(N_HEADS):
            m_old = m_ref[h]
            m_new = jnp.maximum(m_old, jnp.max(logits[h], axis=1, keepdims=True))
            alpha = jnp.exp(m_old - m_new)
            p = jnp.exp(logits[h] - m_new)
            l_ref[h] = alpha * l_ref[h] + jnp.sum(p, axis=1, keepdims=True)
            m_ref[h] = m_new
            probs.append(p.astype(BF16))
            alphas.append(alpha)
        pvs = [lax.dot_general(probs[h], v_of(h), _NT, preferred_element_type=F32)
               for h in range(N_HEADS)]
        for h in range(N_HEADS):
            acc_ref[:, cols[h]] = alphas[h] * acc_ref[:, cols[h]] + pvs[h]

    @pl.when(g == 0)
    def _():
        m_ref[...] = jnp.full(m_ref.shape, -jnp.inf, F32)
        l_ref[...] = jnp.zeros(l_ref.shape, F32)
        acc_ref[...] = jnp.zeros(acc_ref.shape, F32)
        update(lambda h: kn_ref[0, h].astype(BF16), lambda h: vn_ref[0, h].astype(BF16),
               b_ref[0, n_past])

    for j in range(PAGES_PER_STEP):
        keys = slice(j * PAGE_SIZE, (j + 1) * PAGE_SIZE)
        for h in range(N_HEADS):
            kbuf_ref[h, :, keys] = kpages[j][0, h].astype(BF16)
            vbuf_ref[h, :, keys] = vpages[j][0, h].astype(BF16)
    bias = jnp.concatenate([b_ref[0, g * PAGES_PER_STEP + j] for j in range(PAGES_PER_STEP)], axis=1)
    update(lambda h: kbuf_ref[h], lambda h: vbuf_ref[h], bias)

    @pl.when(g == n_steps - 1)
    def _():
        o_ref[0] = _softmax_finish(m_ref, l_ref, acc_ref)


def _page_specs(block):
    zeros = (0,) * (len(block) - 1)

    def spec(j):
        return pl.BlockSpec(block, lambda b, g, pt: (pt[b, g * PAGES_PER_STEP + j],) + zeros)
    return [spec(j) for j in range(PAGES_PER_STEP)]


def _attn_sample(page_table, q_s, qi_s, kw_s, kin_t, kn_t, vn_t, k_pages, v_pages, ki_pages,
                 real_rows, topk):
    db, n_pages = page_table.shape
    assert n_pages % PAGES_PER_STEP == 0
    n_steps = n_pages // PAGES_PER_STEP
    nblk = n_pages + 1
    keys_per_step = PAGES_PER_STEP * PAGE_SIZE
    per_seq = lambda *blk: pl.BlockSpec((1,) + blk, lambda b, g, pt: (b,) + (0,) * len(blk))
    bias = pl.pallas_call(
        functools.partial(_select_sample_kernel, n_steps=n_steps, real_rows=real_rows, topk=topk),
        grid_spec=pltpu.PrefetchScalarGridSpec(
            num_scalar_prefetch=1,
            grid=(db, n_steps),
            in_specs=[per_seq(SAMPLE_ROWS, IDX_HEADS * IDX_DIM), per_seq(SAMPLE_ROWS, LANES),
                      per_seq(IDX_DIM, PAGE_SIZE)] + _page_specs((1, IDX_DIM, PAGE_SIZE)),
            out_specs=per_seq(nblk, SAMPLE_ROWS, LANES),
            scratch_shapes=[pltpu.VMEM((nblk, SAMPLE_ROWS, LANES), F32),
                            pltpu.VMEM((IDX_DIM, keys_per_step), BF16)]),
        out_shape=jax.ShapeDtypeStruct((db, nblk, SAMPLE_ROWS, LANES), F32),
        compiler_params=pltpu.CompilerParams(dimension_semantics=("arbitrary", "arbitrary")),
        name="select_sample",
    )(page_table, qi_s, kw_s, kin_t, *([ki_pages] * PAGES_PER_STEP))

    head_page = (1, N_HEADS, HEAD_DIM, PAGE_SIZE)
    return pl.pallas_call(
        _attn_sample_kernel,
        grid_spec=pltpu.PrefetchScalarGridSpec(
            num_scalar_prefetch=1,
            grid=(db, n_steps),
            in_specs=[per_seq(SAMPLE_ROWS, ATTN_WIDTH), per_seq(nblk, SAMPLE_ROWS, LANES),
                      per_seq(N_HEADS, HEAD_DIM, PAGE_SIZE), per_seq(N_HEADS, HEAD_DIM, PAGE_SIZE)]
            + _page_specs(head_page) + _page_specs(head_page),
            out_specs=per_seq(SAMPLE_ROWS, ATTN_WIDTH),
            scratch_shapes=[pltpu.VMEM((N_HEADS, HEAD_DIM, keys_per_step), BF16),
                            pltpu.VMEM((N_HEADS, HEAD_DIM, keys_per_step), BF16),
                            pltpu.VMEM((N_HEADS, SAMPLE_ROWS, 1), F32),
                            pltpu.VMEM((N_HEADS, SAMPLE_ROWS, 1), F32),
                            pltpu.VMEM((SAMPLE_ROWS, ATTN_WIDTH), F32)]),
        out_shape=jax.ShapeDtypeStruct((db, SAMPLE_ROWS, ATTN_WIDTH), F32),
        compiler_params=pltpu.CompilerParams(
            dimension_semantics=("arbitrary", "arbitrary"), vmem_limit_bytes=VMEM_LIMIT),
        name="attn_sample",
    )(page_table, q_s, bias, kn_t, vn_t,
      *([k_pages] * PAGES_PER_STEP), *([v_pages] * PAGES_PER_STEP))


def _finish_kernel(x_ref, pin_ref, attn_ref, wpool_ref, pscale_ref, wout_ref, ln2_ref,
                   w1_ref, w2_ref, lnf_ref, o_ref):
    pool = jnp.concatenate(
        [jnp.dot(pin_ref[:, g * POOL_GROUP:(g + 1) * POOL_GROUP], wpool_ref[g],
                 preferred_element_type=F32) for g in range(len(POOL_WINDOWS))], axis=1)
    pool = (pool * pscale_ref[...]).astype(BF16)
    mixed = jnp.concatenate([pool, attn_ref[...].astype(BF16)], axis=1)
    h = x_ref[...] + jnp.dot(mixed, wout_ref[...], preferred_element_type=F32)
    hn = _rmsnorm(h, ln2_ref[...]).astype(BF16)
    y = h
    fc = 1024
    for c in range(D_FF // fc):
        a = jnp.dot(hn, w1_ref[:, c * fc:(c + 1) * fc], preferred_element_type=F32)
        a = jnp.square(jnp.maximum(a, 0.0)).astype(BF16)
        y = y + jnp.dot(a, w2_ref[c * fc:(c + 1) * fc, :], preferred_element_type=F32)
    o_ref[...] = _rmsnorm(y, lnf_ref[...])


def _finish(x2d, pin, attn, wpool, pscale, wout, ln2, w1, w2, lnf, tm):
    n = x2d.shape[0]
    row = lambda w: pl.BlockSpec((tm, w), lambda i: (i, 0))
    const = lambda shape: pl.BlockSpec(shape, lambda i: (0,) * len(shape))
    return pl.pallas_call(
        _finish_kernel,
        grid=(n // tm,),
        in_specs=[row(D_MODEL), row(POOL_WIDTH), row(ATTN_WIDTH),
                  const(wpool.shape), const(pscale.shape), const(wout.shape), const(ln2.shape),
                  const(w1.shape), const(w2.shape), const(lnf.shape)],
        out_specs=row(D_MODEL),
        out_shape=jax.ShapeDtypeStruct((n, D_MODEL), F32),
        compiler_params=pltpu.CompilerParams(
            dimension_semantics=("arbitrary",), vmem_limit_bytes=VMEM_LIMIT),
        name="finish",
    )(x2d, pin, attn, wpool, pscale, wout, ln2, w1, w2, lnf)


def kernel(x_prompt, x_sample, cache_k, cache_v, cache_kidx, state_pool, page_table,
           ln1, w_in, w_pool, pool_scale, w_out, ln2, w_ff1, w_ff2, ln_f):
    depth = ln1.shape[0]
    assert depth == 1
    b, t, d = x_prompt.shape
    db, dt, _ = x_sample.shape
    n_pool = cache_k.shape[1]
    past = page_table.shape[1] * PAGE_SIZE
    l = 0

    w_in_bf = jnp.pad(w_in[l], ((0, 0), (0, IN_PAD - IN_WIDTH))).astype(BF16)
    wpool_bf = w_pool[l].astype(BF16)
    wout_bf = w_out[l].astype(BF16)
    w1_bf = w_ff1[l].astype(BF16)
    w2_bf = w_ff2[l].astype(BF16)
    ln1_l = ln1[l].reshape(1, d)
    ln2_l = ln2[l].reshape(1, d)
    lnf_l = ln_f.reshape(1, d)
    pscale_l = pool_scale[l].reshape(1, POOL_WIDTH)

    xp = x_prompt.reshape(b * t, d)
    u, k_t, v_t, ki_t, kb, kib, qt, qit, vtb, kwt = _project(xp, ln1_l, w_in_bf, ROW_TILE, seq_len=t)
    pin = _pool_prompt(u, t, ROW_TILE)
    nc = t // KEY_CHUNK
    attn = _attn_prompt(
        qt, qit, kwt, kib.reshape(b, nc, KEY_CHUNK, IDX_DIM),
        kb.reshape(b, nc, KEY_CHUNK, ATTN_WIDTH), vtb.reshape(b, nc, HEAD_PAIRS * V_EXT, KEY_CHUNK),
        b, t, min(TOPK_MAX, t // 4))
    y_prompt = _finish(xp, pin, attn.reshape(b * t, ATTN_WIDTH), wpool_bf, pscale_l, wout_bf,
                       ln2_l, w1_bf, w2_bf, lnf_l, FINISH_TILE).reshape(b, t, d)
    new_k_prompt = jnp.transpose(k_t.reshape(b, N_HEADS, HEAD_DIM, t), (0, 3, 1, 2))[None]
    new_v_prompt = jnp.transpose(v_t.reshape(b, N_HEADS, HEAD_DIM, t), (0, 3, 1, 2))[None]
    new_kidx_prompt = jnp.transpose(ki_t, (0, 2, 1))[None]
    new_pool_prompt = u.reshape(b, t, POOL_WIDTH)[:, t - POOL_HIST:][None]

    ns = db * dt
    xs = x_sample.reshape(ns, d)
    u_s, k_s, v_s, ki_s, kw_s, qb_s, qib_s = _project(xs, ln1_l, w_in_bf, ns)
    hist = state_pool[l]
    u_s3 = u_s.reshape(db, dt, POOL_WIDTH)
    pin_s = _pool_sample(jnp.swapaxes(u_s3, 0, 1), jnp.swapaxes(hist, 0, 1), past)
    pin_s = jnp.swapaxes(pin_s, 0, 1).reshape(ns, POOL_WIDTH)

    def pad_rows(a, rows):
        a = a.reshape(db, dt, a.shape[-1]).astype(F32)
        return jnp.pad(a, ((0, 0), (0, rows - dt), (0, 0)))

    def tokens_last(a, *mid):
        a = jnp.moveaxis(a.reshape((db, dt) + mid), 1, -1)
        return jnp.pad(a, ((0, 0),) * (a.ndim - 1) + ((0, PAGE_SIZE - dt),))

    attn_s = _attn_sample(
        page_table,
        pad_rows(qb_s, SAMPLE_ROWS),
        pad_rows(qib_s, SAMPLE_ROWS),
        pad_rows(kw_s, SAMPLE_ROWS),
        tokens_last(ki_s, IDX_DIM), tokens_last(k_s, N_HEADS, HEAD_DIM),
        tokens_last(v_s, N_HEADS, HEAD_DIM),
        jnp.transpose(cache_k[l], (0, 2, 3, 1)), jnp.transpose(cache_v[l], (0, 2, 3, 1)),
        jnp.transpose(cache_kidx[l], (0, 2, 1)), dt, min(TOPK_MAX, (past + dt) // 4))
    attn_s = attn_s[:, :dt].reshape(ns, ATTN_WIDTH)
    y_sample = _finish(xs, pin_s, attn_s, wpool_bf, pscale_l, wout_bf,
                       ln2_l, w1_bf, w2_bf, lnf_l, ns).reshape(db, dt, d)
    new_k_sample = k_s.reshape(1, db, dt, N_HEADS, HEAD_DIM)
    new_v_sample = v_s.reshape(1, db, dt, N_HEADS, HEAD_DIM)
    new_kidx_sample = ki_s.reshape(1, db, dt, IDX_DIM)
    new_pool_sample = jnp.concatenate([hist, u_s3], axis=1)[:, -POOL_HIST:][None]

    return (y_prompt, y_sample, new_k_prompt, new_v_prompt, new_kidx_prompt, new_pool_prompt,
            new_k_sample, new_v_sample, new_kidx_sample, new_pool_sample)
```

```python
import functools

import jax
import jax.numpy as jnp
from jax import lax
from jax.experimental import pallas as pl
from jax.experimental.pallas import tpu as pltpu

D_MODEL = 1024
POOL_WIDTH = 512
POOL_WINDOWS = (2, 4, 8, 16)
POOL_GROUP = 128
POOL_HIST = 15
ATTN_WIDTH = 512
HEAD_DIM = 64
N_HEADS = 8
IDX_HEADS = 4
IDX_DIM = 64
TOPK_MAX = 256
PAGE_SIZE = 128
D_FF = 4096
NORM_EPS = 1e-6
NEG_INF = -1e30
IN_WIDTH = 2372
IN_PAD = 2432
KW_OFF = 2304

LANES = 128
SUBLANES = 8
ROW_TILE = 512
FINISH_TILE = 256
Q_TILE = LANES
KEY_CHUNK = 512
KEY_SUB = 512
HEAD_PAIRS = N_HEADS // 2
V_EXT = 2 * HEAD_DIM + 16
LOG2_E = 1.4426950408889634
PAGES_PER_STEP = 8
SAMPLE_ROWS = 8
VMEM_LIMIT = 56 * 1024 * 1024

F32 = jnp.float32
BF16 = jnp.bfloat16


def _rmsnorm(x, g):
    return x * lax.rsqrt(jnp.mean(x * x, axis=-1, keepdims=True) + NORM_EPS) * g


def _proj_split(x_ref, ln_ref, w_ref):
    xn = _rmsnorm(x_ref[...], ln_ref[...]).astype(BF16)
    z = jnp.dot(xn, w_ref[...], preferred_element_type=F32)
    return (z[:, 0:512], z[:, 512:1024] * (HEAD_DIM ** -0.5), z[:, 1024:1536], z[:, 1536:2048],
            z[:, 2048:KW_OFF] * (IDX_DIM ** -0.5), z[:, KW_OFF:IN_PAD])


def _proj_rows_kernel(x_ref, ln_ref, w_ref, u_ref, k_ref, v_ref, ki_ref, kw_ref, qb_ref, qib_ref):
    u, q, k, v, qi, kw = _proj_split(x_ref, ln_ref, w_ref)
    u_ref[...] = u
    k_ref[...] = k
    v_ref[...] = v
    ki_ref[...] = kw[:, :IDX_DIM]
    kw_ref[...] = kw
    qb_ref[...] = q.astype(BF16)
    qib_ref[...] = qi.astype(BF16)


def _proj_cols_kernel(x_ref, ln_ref, w_ref, u_ref, kt_ref, vt_ref, kit_ref,
                      kb_ref, kib_ref, qtb_ref, qitb_ref, vtb_ref, kwt_ref):
    u, q, k, v, qi, kw = _proj_split(x_ref, ln_ref, w_ref)
    u_ref[...] = u
    v_t = v.T
    kw_t = kw.T
    kt_ref[0] = k.T
    vt_ref[0] = v_t
    kit_ref[0] = kw_t[:IDX_DIM, :]
    kb_ref[...] = k.astype(BF16)
    kib_ref[...] = kw[:, :IDX_DIM].astype(BF16)
    qtb_ref[0] = (q * LOG2_E).T.astype(BF16)
    qitb_ref[0] = qi.T.astype(BF16)
    ones = jnp.ones((V_EXT - 2 * HEAD_DIM, v_t.shape[1]), F32)
    pieces = []
    for p in range(HEAD_PAIRS):
        pieces += [v_t[p * 2 * HEAD_DIM:(p + 1) * 2 * HEAD_DIM], ones]
    vtb_ref[0] = jnp.concatenate(pieces, axis=0).astype(BF16)
    kwt_ref[0] = kw_t


def _project(x2d, ln, w_bf, tm, seq_len=None):
    n = x2d.shape[0]
    nt = n // tm
    row = lambda w: pl.BlockSpec((tm, w), lambda i: (i, 0))
    qiw = IDX_HEADS * IDX_DIM
    sds = jax.ShapeDtypeStruct
    if seq_len is None:
        body = _proj_rows_kernel
        out_shape = (sds((n, POOL_WIDTH), F32), sds((n, ATTN_WIDTH), F32), sds((n, ATTN_WIDTH), F32),
                     sds((n, IDX_DIM), F32), sds((n, LANES), F32),
                     sds((n, ATTN_WIDTH), BF16), sds((n, qiw), BF16))
        out_specs = (row(POOL_WIDTH), row(ATTN_WIDTH), row(ATTN_WIDTH), row(IDX_DIM), row(LANES),
                     row(ATTN_WIDTH), row(qiw))
    else:
        body = _proj_cols_kernel
        tps = seq_len // tm
        nb = n // seq_len
        seq = lambda r: pl.BlockSpec((1, r, tm), lambda i: (i // tps, 0, i % tps))
        col = lambda r: pl.BlockSpec((1, r, tm), lambda i: (i, 0, 0))
        out_shape = (sds((n, POOL_WIDTH), F32),
                     sds((nb, ATTN_WIDTH, seq_len), F32),
                     sds((nb, ATTN_WIDTH, seq_len), F32),
                     sds((nb, IDX_DIM, seq_len), F32),
                     sds((n, ATTN_WIDTH), BF16), sds((n, IDX_DIM), BF16),
                     sds((nt, ATTN_WIDTH, tm), BF16),
                     sds((nt, qiw, tm), BF16),
                     sds((nt, HEAD_PAIRS * V_EXT, tm), BF16),
                     sds((nt, LANES, tm), F32))
        out_specs = (row(POOL_WIDTH), seq(ATTN_WIDTH), seq(ATTN_WIDTH), seq(IDX_DIM),
                     row(ATTN_WIDTH), row(IDX_DIM),
                     col(ATTN_WIDTH), col(qiw), col(HEAD_PAIRS * V_EXT), col(LANES))
    return pl.pallas_call(
        body,
        grid=(nt,),
        in_specs=[row(D_MODEL),
                  pl.BlockSpec((1, D_MODEL), lambda i: (0, 0)),
                  pl.BlockSpec((D_MODEL, IN_PAD), lambda i: (0, 0))],
        out_specs=out_specs,
        out_shape=out_shape,
        compiler_params=pltpu.CompilerParams(
            dimension_semantics=("arbitrary",), vmem_limit_bytes=VMEM_LIMIT),
        name="proj",
    )(x2d, ln, w_bf)


def _pool_prompt_kernel(u_ref, halo_ref, o_ref, ext_ref, *, tm, tiles_per_seq):
    it = pl.program_id(0) % tiles_per_seq
    ext_ref[0:16, :] = jnp.where(it == 0, 0.0, halo_ref[...])
    ext_ref[16:, :] = u_ref[...]
    pos = it * tm + lax.broadcasted_iota(jnp.int32, (tm, 1), 0)
    for g, w in enumerate(POOL_WINDOWS):
        cols = slice(g * POOL_GROUP, (g + 1) * POOL_GROUP)
        cur = ext_ref[16:16 + tm, cols]
        s = cur
        for j in range(1, w):
            s = s + ext_ref[16 - j:16 - j + tm, cols]
        cnt = jnp.minimum(pos + 1, w).astype(F32)
        o_ref[:, cols] = (s / cnt - cur).astype(BF16)


def _pool_prompt(u2d, seq_len, tm):
    n = u2d.shape[0]
    hb = tm // 16
    return pl.pallas_call(
        functools.partial(_pool_prompt_kernel, tm=tm, tiles_per_seq=seq_len // tm),
        grid=(n // tm,),
        in_specs=[pl.BlockSpec((tm, POOL_WIDTH), lambda i: (i, 0)),
                  pl.BlockSpec((16, POOL_WIDTH), lambda i: (jnp.maximum(i * hb - 1, 0), 0))],
        out_specs=pl.BlockSpec((tm, POOL_WIDTH), lambda i: (i, 0)),
        out_shape=jax.ShapeDtypeStruct((n, POOL_WIDTH), BF16),
        scratch_shapes=[pltpu.VMEM((tm + 16, POOL_WIDTH), F32)],
        compiler_params=pltpu.CompilerParams(dimension_semantics=("arbitrary",)),
        name="pool_prompt",
    )(u2d, u2d)


def _pool_sample_kernel(u_ref, h_ref, o_ref, *, start_pos):
    t = u_ref.shape[0]

    def row(r, cols):
        return h_ref[POOL_HIST + r, :, cols] if r < 0 else u_ref[r, :, cols]

    for i in range(t):
        for g, w in enumerate(POOL_WINDOWS):
            cols = slice(g * POOL_GROUP, (g + 1) * POOL_GROUP)
            cur = row(i, cols)
            s = cur
            for j in range(1, w):
                s = s + row(i - j, cols)
            cnt = float(min(start_pos + i + 1, w))
            o_ref[i, :, cols] = (s / cnt - cur).astype(BF16)


def _pool_sample(u_t, hist_t, start_pos):
    return pl.pallas_call(
        functools.partial(_pool_sample_kernel, start_pos=start_pos),
        out_shape=jax.ShapeDtypeStruct(u_t.shape, BF16),
        name="pool_sample",
    )(u_t, hist_t)


def _key_to_f32(key):
    bits = key ^ ((key >> 31) & 0x7FFFFFFF)
    return lax.bitcast_convert_type(bits, F32)


def _kth_largest_key(count_ge, shape, kf):
    def bit_step(it, st):
        tau, n_tau = st
        cand = tau ^ lax.shift_left(jnp.int32(1), 31 - it)
        cnt = count_ge(cand)
        ok = cnt >= kf
        return jnp.where(ok, cand, tau), jnp.where(ok, cnt, n_tau)

    return lax.fori_loop(0, 32, bit_step,
                         (jnp.full(shape, -2 ** 31, jnp.int32), jnp.full(shape, jnp.inf, F32)))


def _select_to_bias(s_ref, nch, ch, rows, real_rows, k, idx_bits):
    lane = lax.broadcasted_iota(jnp.int32, (rows, LANES), 1)
    kf = float(k)
    n_acc = min(4, ch)

    def count(pred):
        def body(c, accs):
            accs = list(accs)
            for u in range(ch):
                j = c * ch + u
                accs[u % n_acc] = accs[u % n_acc] + jnp.where(pred(s_ref[j], j), 1.0, 0.0)
            return tuple(accs)
        accs = lax.fori_loop(0, nch, body,
                             tuple(jnp.zeros((rows, LANES), F32) for _ in range(n_acc)))
        return jnp.sum(sum(accs[1:], accs[0]), axis=1, keepdims=True)

    def bcast(key):
        return jnp.broadcast_to(_key_to_f32(key), (rows, LANES))

    def count_ge(key):
        cb = bcast(key)
        return count(lambda blk, j: blk >= cb)

    tau, n_ge = _kth_largest_key(count_ge, (rows, 1), kf)
    tb = bcast(tau)
    need = kf - count(lambda blk, j: blk > tb)

    def index_search():
        def index_step(it, p):
            cand = p | lax.shift_left(jnp.int32(1), idx_bits - 1 - it)
            cb = jnp.broadcast_to(cand, (rows, LANES))
            cnt = count(lambda blk, j: (blk == tb) & ((lane + j * LANES) < cb))
            return jnp.where(cnt < need, cand, p)
        return lax.fori_loop(0, idx_bits, index_step, jnp.zeros((rows, 1), jnp.int32))

    real = lax.broadcasted_iota(jnp.int32, (rows, 1), 0) < real_rows
    any_tie = jnp.max(jnp.where((n_ge > kf) & real, 1.0, 0.0)) > 0.0
    cut = lax.cond(any_tie, index_search,
                   lambda: jnp.full((rows, 1), 2 ** 30, jnp.int32))
    cutb = jnp.broadcast_to(cut, (rows, LANES))

    def write(c, carry):
        for u in range(ch):
            j = c * ch + u
            blk = s_ref[j]
            keep = (blk > tb) | ((blk == tb) & ((lane + j * LANES) <= cutb))
            s_ref[j] = jnp.where(keep, 0.0, NEG_INF)
        return carry
    lax.fori_loop(0, nch, write, 0)


def _select_to_bias_t(s_ref, tri_ref, nch, k):
    ck = s_ref.shape[1]
    kf = float(k)

    n_acc = 8
    rows_per_acc = ck // n_acc

    def count(pred):
        def body(c, accs):
            ind = jnp.where(pred(s_ref[c]), 1.0, 0.0)
            out = []
            for a in range(n_acc):
                part = ind[a * rows_per_acc:(a + 1) * rows_per_acc]
                out.append(accs[a] + jnp.sum(
                    part.reshape(rows_per_acc // SUBLANES, SUBLANES, LANES), axis=0))
            return tuple(out)
        accs = lax.fori_loop(0, nch, body,
                             tuple(jnp.zeros((SUBLANES, LANES), F32) for _ in range(n_acc)))
        return jnp.sum(sum(accs[1:], accs[0]), axis=0, keepdims=True)

    def count_ge(key):
        cf = _key_to_f32(key)
        return count(lambda blk: blk >= cf)

    tau, n_ge = _kth_largest_key(count_ge, (1, LANES), kf)
    tf = _key_to_f32(tau)
    excess = n_ge - kf

    def write(i, seen):
        c = nch - 1 - i
        blk = s_ref[c]
        tied = blk == tf
        after = seen + jnp.dot(tri_ref[...], jnp.where(tied, 1.0, 0.0).astype(BF16),
                               preferred_element_type=F32)
        keep = (blk > tf) | (tied & (after > excess))
        s_ref[c] = jnp.where(keep, 0.0, NEG_INF)
        return after[0:1, :]
    lax.fori_loop(0, nch, write, jnp.zeros((1, LANES), F32))


_NT = (((1,), (1,)), ((), ()))


def _softmax_finish(m_ref, l_ref, acc_ref):
    return jnp.concatenate(
        [acc_ref[:, h * HEAD_DIM:(h + 1) * HEAD_DIM] / l_ref[h] for h in range(N_HEADS)], axis=1)


def _attn_prompt_kernel(qt_ref, qit_ref, kwt_ref, kib_ref, kb_ref, vt_ref, tri_ref, o_ref,
                        s_ref, wq_ref, m_ref, acc_ref, *, topk):
    qb = pl.program_id(1)
    nch = qb // (KEY_CHUNK // Q_TILE) + 1
    q_pos = qb * Q_TILE + lax.broadcasted_iota(jnp.int32, (KEY_CHUNK, LANES), 1)
    key_row = lax.broadcasted_iota(jnp.int32, (KEY_CHUNK, LANES), 0)

    qit = qit_ref[0]
    w_rows = [kwt_ref[0, IDX_DIM + h:IDX_DIM + h + 1, :] * (IDX_HEADS ** -0.5)
              for h in range(IDX_HEADS)]
    wqi = [jnp.concatenate([qit[(2 * p) * IDX_DIM:(2 * p + 1) * IDX_DIM, :],
                            qit[(2 * p + 1) * IDX_DIM:(2 * p + 2) * IDX_DIM, :]], axis=1)
           for p in range(IDX_HEADS // 2)]

    def score_chunk(c, carry):
        ki = kib_ref[0, c]
        acc = jnp.zeros((KEY_CHUNK, LANES), F32)
        rs = [jnp.dot(ki, wqi[p], preferred_element_type=F32) for p in range(IDX_HEADS // 2)]
        for p, r in enumerate(rs):
            acc = acc + w_rows[2 * p] * jnp.maximum(r[:, :LANES], 0.0)
            acc = acc + w_rows[2 * p + 1] * jnp.maximum(r[:, LANES:], 0.0)
        s_ref[c] = jnp.where(key_row + c * KEY_CHUNK <= q_pos, acc, NEG_INF)
        return carry
    lax.fori_loop(0, nch, score_chunk, 0)

    @pl.when((qb + 1) * Q_TILE <= topk)
    def _():
        def write(c, carry):
            s_ref[c] = jnp.where(key_row + c * KEY_CHUNK <= q_pos, 0.0, NEG_INF)
            return carry
        lax.fori_loop(0, nch, write, 0)

    @pl.when((qb + 1) * Q_TILE > topk)
    def _():
        _select_to_bias_t(s_ref, tri_ref, nch, topk)

    half = lax.broadcasted_iota(jnp.int32, (2 * HEAD_DIM, LANES), 0) < HEAD_DIM
    for p in range(HEAD_PAIRS):
        qp = qt_ref[0, p * 2 * HEAD_DIM:(p + 1) * 2 * HEAD_DIM, :]
        zero = jnp.zeros_like(qp)
        wq_ref[p] = jnp.concatenate([jnp.where(half, qp, zero), jnp.where(half, zero, qp)], axis=1)
    m_ref[...] = jnp.full(m_ref.shape, -jnp.inf, F32)
    acc_ref[...] = jnp.zeros(acc_ref.shape, F32)

    blocks = [(sub, p) for sub in range(KEY_CHUNK // KEY_SUB) for p in range(HEAD_PAIRS)]
    keys_of = lambda sub: slice(sub * KEY_SUB, (sub + 1) * KEY_SUB)
    dims_of = lambda p: slice(p * 2 * HEAD_DIM, (p + 1) * 2 * HEAD_DIM)
    vrows_of = lambda p: slice(p * V_EXT, (p + 1) * V_EXT)

    def logits(c, sub, p):
        return jnp.dot(kb_ref[0, c, keys_of(sub), dims_of(p)], wq_ref[p],
                       preferred_element_type=F32)

    def attn_chunk(c, carry):
        st_next = logits(c, *blocks[0])
        pending = None
        for i, (sub, p) in enumerate(blocks):
            st = st_next
            if i + 1 < len(blocks):
                st_next = logits(c, *blocks[i + 1])
            bias = s_ref[c, keys_of(sub), :]
            probs, alphas = [], []
            for hh in range(2):
                h = 2 * p + hh
                s = st[:, hh * LANES:(hh + 1) * LANES] + bias
                m_old = m_ref[h]
                m_new = jnp.maximum(m_old, jnp.max(s, axis=0, keepdims=True))
                alphas.append(jnp.exp2(m_old - m_new))
                probs.append(jnp.exp2(s - m_new).astype(BF16))
                m_ref[h] = m_new
            pv = jnp.dot(vt_ref[0, c, vrows_of(p), keys_of(sub)], jnp.concatenate(probs, axis=1),
                         preferred_element_type=F32)
            if pending is not None:
                pp, pa, ppv = pending
                acc_ref[pp] = acc_ref[pp] * pa + ppv
            pending = (p, jnp.concatenate(alphas, axis=1), pv)
        pp, pa, ppv = pending
        acc_ref[pp] = acc_ref[pp] * pa + ppv
        return carry
    lax.fori_loop(0, nch, attn_chunk, 0)

    outs = []
    for h in range(N_HEADS):
        p, hh = divmod(h, 2)
        lanes = slice(hh * LANES, (hh + 1) * LANES)
        o_t = acc_ref[p, hh * HEAD_DIM:(hh + 1) * HEAD_DIM, lanes]
        outs.append(o_t / acc_ref[p, 2 * HEAD_DIM:2 * HEAD_DIM + 1, lanes])
    o_ref[0] = jnp.concatenate(outs, axis=0).T.astype(BF16)


def _attn_prompt(qt, qit, kwt, kib, kb, vt, b, t, topk):
    nc = t // KEY_CHUNK
    tiles_per_seq = t // ROW_TILE
    qpt = ROW_TILE // Q_TILE
    assert t % KEY_CHUNK == 0 and t <= 2 ** 13 and KEY_CHUNK == ROW_TILE
    tcol = lambda r: pl.BlockSpec((1, r, Q_TILE),
                                  lambda i, j: (i * tiles_per_seq + j // qpt, 0, j % qpt))
    full = lambda r, c: pl.BlockSpec((1, nc, r, c), lambda i, j: (i, 0, 0, 0))
    return pl.pallas_call(
        functools.partial(_attn_prompt_kernel, topk=topk),
        grid=(b, t // Q_TILE),
        in_specs=[tcol(ATTN_WIDTH), tcol(IDX_HEADS * IDX_DIM), tcol(LANES),
                  full(KEY_CHUNK, IDX_DIM), full(KEY_CHUNK, ATTN_WIDTH),
                  full(HEAD_PAIRS * V_EXT, KEY_CHUNK),
                  pl.BlockSpec((KEY_CHUNK, KEY_CHUNK), lambda i, j: (0, 0))],
        out_specs=pl.BlockSpec((1, Q_TILE, ATTN_WIDTH), lambda i, j: (i, j, 0)),
        out_shape=jax.ShapeDtypeStruct((b, t, ATTN_WIDTH), BF16),
        scratch_shapes=[pltpu.VMEM((nc, KEY_CHUNK, LANES), F32),
                        pltpu.VMEM((HEAD_PAIRS, 2 * HEAD_DIM, 2 * LANES), BF16),
                        pltpu.VMEM((N_HEADS, 1, LANES), F32),
                        pltpu.VMEM((HEAD_PAIRS, V_EXT, 2 * LANES), F32)],
        compiler_params=pltpu.CompilerParams(
            dimension_semantics=("arbitrary", "arbitrary"), vmem_limit_bytes=VMEM_LIMIT),
        name="attn_prompt",
    )(qt, qit, kwt, kib, kb, vt, jnp.triu(jnp.ones((KEY_CHUNK, KEY_CHUNK), BF16)))


def _select_sample_kernel(pt_ref, qi_ref, kw_ref, kin_ref, *rest, n_steps, real_rows, topk):
    pages = rest[:PAGES_PER_STEP]
    o_ref, s_ref, kbuf_ref = rest[PAGES_PER_STEP:]
    g = pl.program_id(1)
    rows = SAMPLE_ROWS
    qi = qi_ref[0].astype(BF16)
    kw = kw_ref[0]
    w_cols = [kw[:, IDX_DIM + h:IDX_DIM + h + 1] * (IDX_HEADS ** -0.5) for h in range(IDX_HEADS)]

    def scores(keys_t):
        acc = jnp.zeros((rows, keys_t.shape[1]), F32)
        ss = [jnp.dot(qi[:, h * IDX_DIM:(h + 1) * IDX_DIM], keys_t, preferred_element_type=F32)
              for h in range(IDX_HEADS)]
        for h in range(IDX_HEADS):
            acc = acc + w_cols[h] * jnp.maximum(ss[h], 0.0)
        return acc

    for j in range(PAGES_PER_STEP):
        kbuf_ref[:, j * PAGE_SIZE:(j + 1) * PAGE_SIZE] = pages[j][0].astype(BF16)
    past = scores(kbuf_ref[...])
    for j in range(PAGES_PER_STEP):
        s_ref[g * PAGES_PER_STEP + j] = past[:, j * PAGE_SIZE:(j + 1) * PAGE_SIZE]

    n_past = n_steps * PAGES_PER_STEP

    @pl.when(g == 0)
    def _():
        new = scores(kin_ref[0].astype(BF16))
        lane = lax.broadcasted_iota(jnp.int32, (rows, LANES), 1)
        row = lax.broadcasted_iota(jnp.int32, (rows, LANES), 0)
        s_ref[n_past] = jnp.where(lane <= row, new, NEG_INF)

    @pl.when(g == n_steps - 1)
    def _():
        _select_to_bias(s_ref, 1, n_past + 1, rows, real_rows, topk, 14)
        o_ref[0] = s_ref[...]


def _attn_sample_kernel(pt_ref, q_ref, b_ref, kn_ref, vn_ref, *rest):
    kpages = rest[:PAGES_PER_STEP]
    vpages = rest[PAGES_PER_STEP:2 * PAGES_PER_STEP]
    o_ref, kbuf_ref, vbuf_ref, m_ref, l_ref, acc_ref = rest[2 * PAGES_PER_STEP:]
    g = pl.program_id(1)
    n_steps = pl.num_programs(1)
    n_past = b_ref.shape[1] - 1
    q = q_ref[0].astype(BF16)

    def update(k_of, v_of, bias):
        cols = [slice(h * HEAD_DIM, (h + 1) * HEAD_DIM) for h in range(N_HEADS)]
        logits = [jnp.dot(q[:, cols[h]], k_of(h), preferred_element_type=F32) + bias
                  for h in range(N_HEADS)]
        probs, alphas = [], []
        for h in range(N_HEADS):
            m_old = m_ref[h]
            m_new = jnp.maximum(m_old, jnp.max(logits[h], axis=1, keepdims=True))
            alpha = jnp.exp(m_old - m_new)
            p = jnp.exp(logits[h] - m_new)
            l_ref[h] = alpha * l_ref[h] + jnp.sum(p, axis=1, keepdims=True)
            m_ref[h] = m_new
            probs.append(p.astype(BF16))
            alphas.append(alpha)
        pvs = [lax.dot_general(probs[h], v_of(h), _NT, preferred_element_type=F32)
               for h in range(N_HEADS)]
        for h in range(N_HEADS):
            acc_ref[:, cols[h]] = alphas[h] * acc_ref[:, cols[h]] + pvs[h]

    @pl.when(g == 0)
    def _():
        m_ref[...] = jnp.full(m_ref.shape, -jnp.inf, F32)
        l_ref[...] = jnp.zeros(l_ref.shape, F32)
        acc_ref[...] = jnp.zeros(acc_ref.shape, F32)
        update(lambda h: kn_ref[0, h].astype(BF16), lambda h: vn_ref[0, h].astype(BF16),
               b_ref[0, n_past])

    for j in range(PAGES_PER_STEP):
        keys = slice(j * PAGE_SIZE, (j + 1) * PAGE_SIZE)
        for h in range(N_HEADS):
            kbuf_ref[h, :, keys] = kpages[j][0, h].astype(BF16)
            vbuf_ref[h, :, keys] = vpages[j][0, h].astype(BF16)
    bias = jnp.concatenate([b_ref[0, g * PAGES_PER_STEP + j] for j in range(PAGES_PER_STEP)], axis=1)
    update(lambda h: kbuf_ref[h], lambda h: vbuf_ref[h], bias)

    @pl.when(g == n_steps - 1)
    def _():
        o_ref[0] = _softmax_finish(m_ref, l_ref, acc_ref)


def _page_specs(block):
    zeros = (0,) * (len(block) - 1)

    def spec(j):
        return pl.BlockSpec(block, lambda b, g, pt: (pt[b, g * PAGES_PER_STEP + j],) + zeros)
    return [spec(j) for j in range(PAGES_PER_STEP)]


def _attn_sample(page_table, q_s, qi_s, kw_s, kin_t, kn_t, vn_t, k_pages, v_pages, ki_pages,
                 real_rows, topk):
    db, n_pages = page_table.shape
    assert n_pages % PAGES_PER_STEP == 0
    n_steps = n_pages // PAGES_PER_STEP
    nblk = n_pages + 1
    keys_per_step = PAGES_PER_STEP * PAGE_SIZE
    per_seq = lambda *blk: pl.BlockSpec((1,) + blk, lambda b, g, pt: (b,) + (0,) * len(blk))
    bias = pl.pallas_call(
        functools.partial(_select_sample_kernel, n_steps=n_steps, real_rows=real_rows, topk=topk),
        grid_spec=pltpu.PrefetchScalarGridSpec(
            num_scalar_prefetch=1,
            grid=(db, n_steps),
            in_specs=[per_seq(SAMPLE_ROWS, IDX_HEADS * IDX_DIM), per_seq(SAMPLE_ROWS, LANES),
                      per_seq(IDX_DIM, PAGE_SIZE)] + _page_specs((1, IDX_DIM, PAGE_SIZE)),
            out_specs=per_seq(nblk, SAMPLE_ROWS, LANES),
            scratch_shapes=[pltpu.VMEM((nblk, SAMPLE_ROWS, LANES), F32),
                            pltpu.VMEM((IDX_DIM, keys_per_step), BF16)]),
        out_shape=jax.ShapeDtypeStruct((db, nblk, SAMPLE_ROWS, LANES), F32),
        compiler_params=pltpu.CompilerParams(dimension_semantics=("arbitrary", "arbitrary")),
        name="select_sample",
    )(page_table, qi_s, kw_s, kin_t, *([ki_pages] * PAGES_PER_STEP))

    head_page = (1, N_HEADS, HEAD_DIM, PAGE_SIZE)
    return pl.pallas_call(
        _attn_sample_kernel,
        grid_spec=pltpu.PrefetchScalarGridSpec(
            num_scalar_prefetch=1,
            grid=(db, n_steps),
            in_specs=[per_seq(SAMPLE_ROWS, ATTN_WIDTH), per_seq(nblk, SAMPLE_ROWS, LANES),
                      per_seq(N_HEADS, HEAD_DIM, PAGE_SIZE), per_seq(N_HEADS, HEAD_DIM, PAGE_SIZE)]
            + _page_specs(head_page) + _page_specs(head_page),
            out_specs=per_seq(SAMPLE_ROWS, ATTN_WIDTH),
            scratch_shapes=[pltpu.VMEM((N_HEADS, HEAD_DIM, keys_per_step), BF16),
                            pltpu.VMEM((N_HEADS, HEAD_DIM, keys_per_step), BF16),
                            pltpu.VMEM((N_HEADS, SAMPLE_ROWS, 1), F32),
                            pltpu.VMEM((N_HEADS, SAMPLE_ROWS, 1), F32),
                            pltpu.VMEM((SAMPLE_ROWS, ATTN_WIDTH), F32)]),
        out_shape=jax.ShapeDtypeStruct((db, SAMPLE_ROWS, ATTN_WIDTH), F32),
        compiler_params=pltpu.CompilerParams(
            dimension_semantics=("arbitrary", "arbitrary"), vmem_limit_bytes=VMEM_LIMIT),
        name="attn_sample",
    )(page_table, q_s, bias, kn_t, vn_t,
      *([k_pages] * PAGES_PER_STEP), *([v_pages] * PAGES_PER_STEP))


def _finish_kernel(x_ref, pin_ref, attn_ref, wpool_ref, pscale_ref, wout_ref, ln2_ref,
                   w1_ref, w2_ref, lnf_ref, o_ref):
    pool = jnp.concatenate(
        [jnp.dot(pin_ref[:, g * POOL_GROUP:(g + 1) * POOL_GROUP], wpool_ref[g],
                 preferred_element_type=F32) for g in range(len(POOL_WINDOWS))], axis=1)
    pool = (pool * pscale_ref[...]).astype(BF16)
    mixed = jnp.concatenate([pool, attn_ref[...].astype(BF16)], axis=1)
    h = x_ref[...] + jnp.dot(mixed, wout_ref[...], preferred_element_type=F32)
    hn = _rmsnorm(h, ln2_ref[...]).astype(BF16)
    y = h
    fc = 1024
    for c in range(D_FF // fc):
        a = jnp.dot(hn, w1_ref[:, c * fc:(c + 1) * fc], preferred_element_type=F32)
        a = jnp.square(jnp.maximum(a, 0.0)).astype(BF16)
        y = y + jnp.dot(a, w2_ref[c * fc:(c + 1) * fc, :], preferred_element_type=F32)
    o_ref[...] = _rmsnorm(y, lnf_ref[...])


def _finish(x2d, pin, attn, wpool, pscale, wout, ln2, w1, w2, lnf, tm):
    n = x2d.shape[0]
    row = lambda w: pl.BlockSpec((tm, w), lambda i: (i, 0))
    const = lambda shape: pl.BlockSpec(shape, lambda i: (0,) * len(shape))
    return pl.pallas_call(
        _finish_kernel,
        grid=(n // tm,),
        in_specs=[row(D_MODEL), row(POOL_WIDTH), row(ATTN_WIDTH),
                  const(wpool.shape), const(pscale.shape), const(wout.shape), const(ln2.shape),
                  const(w1.shape), const(w2.shape), const(lnf.shape)],
        out_specs=row(D_MODEL),
        out_shape=jax.ShapeDtypeStruct((n, D_MODEL), F32),
        compiler_params=pltpu.CompilerParams(
            dimension_semantics=("arbitrary",), vmem_limit_bytes=VMEM_LIMIT),
        name="finish",
    )(x2d, pin, attn, wpool, pscale, wout, ln2, w1, w2, lnf)


def kernel(x_prompt, x_sample, cache_k, cache_v, cache_kidx, state_pool, page_table,
           ln1, w_in, w_pool, pool_scale, w_out, ln2, w_ff1, w_ff2, ln_f):
    depth = ln1.shape[0]
    assert depth == 1
    b, t, d = x_prompt.shape
    db, dt, _ = x_sample.shape
    n_pool = cache_k.shape[1]
    past = page_table.shape[1] * PAGE_SIZE
    l = 0

    w_in_bf = jnp.pad(w_in[l], ((0, 0), (0, IN_PAD - IN_WIDTH))).astype(BF16)
    wpool_bf = w_pool[l].astype(BF16)
    wout_bf = w_out[l].astype(BF16)
    w1_bf = w_ff1[l].astype(BF16)
    w2_bf = w_ff2[l].astype(BF16)
    ln1_l = ln1[l].reshape(1, d)
    ln2_l = ln2[l].reshape(1, d)
    lnf_l = ln_f.reshape(1, d)
    pscale_l = pool_scale[l].reshape(1, POOL_WIDTH)

    xp = x_prompt.reshape(b * t, d)
    u, k_t, v_t, ki_t, kb, kib, qt, qit, vtb, kwt = _project(xp, ln1_l, w_in_bf, ROW_TILE, seq_len=t)
    pin = _pool_prompt(u, t, ROW_TILE)
    nc = t // KEY_CHUNK
    attn = _attn_prompt(
        qt, qit, kwt, kib.reshape(b, nc, KEY_CHUNK, IDX_DIM),
        kb.reshape(b, nc, KEY_CHUNK, ATTN_WIDTH), vtb.reshape(b, nc, HEAD_PAIRS * V_EXT, KEY_CHUNK),
        b, t, min(TOPK_MAX, t // 4))
    y_prompt = _finish(xp, pin, attn.reshape(b * t, ATTN_WIDTH), wpool_bf, pscale_l, wout_bf,
                       ln2_l, w1_bf, w2_bf, lnf_l, FINISH_TILE).reshape(b, t, d)
    new_k_prompt = jnp.transpose(k_t.reshape(b, N_HEADS, HEAD_DIM, t), (0, 3, 1, 2))[None]
    new_v_prompt = jnp.transpose(v_t.reshape(b, N_HEADS, HEAD_DIM, t), (0, 3, 1, 2))[None]
    new_kidx_prompt = jnp.transpose(ki_t, (0, 2, 1))[None]
    new_pool_prompt = u.reshape(b, t, POOL_WIDTH)[:, t - POOL_HIST:][None]

    ns = db * dt
    xs = x_sample.reshape(ns, d)
    u_s, k_s, v_s, ki_s, kw_s, qb_s, qib_s = _project(xs, ln1_l, w_in_bf, ns)
    hist = state_pool[l]
    u_s3 = u_s.reshape(db, dt, POOL_WIDTH)
    pin_s = _pool_sample(jnp.swapaxes(u_s3, 0, 1), jnp.swapaxes(hist, 0, 1), past)
    pin_s = jnp.swapaxes(pin_s, 0, 1).reshape(ns, POOL_WIDTH)

    def pad_rows(a, rows):
        a = a.reshape(db, dt, a.shape[-1]).astype(F32)
        return jnp.pad(a, ((0, 0), (0, rows - dt), (0, 0)))

    def tokens_last(a, *mid):
        a = jnp.moveaxis(a.reshape((db, dt) + mid), 1, -1)
        return jnp.pad(a, ((0, 0),) * (a.ndim - 1) + ((0, PAGE_SIZE - dt),))

    attn_s = _attn_sample(
        page_table,
        pad_rows(qb_s, SAMPLE_ROWS),
        pad_rows(qib_s, SAMPLE_ROWS),
        pad_rows(kw_s, SAMPLE_ROWS),
        tokens_last(ki_s, IDX_DIM), tokens_last(k_s, N_HEADS, HEAD_DIM),
        tokens_last(v_s, N_HEADS, HEAD_DIM),
        jnp.transpose(cache_k[l], (0, 2, 3, 1)), jnp.transpose(cache_v[l], (0, 2, 3, 1)),
        jnp.transpose(cache_kidx[l], (0, 2, 1)), dt, min(TOPK_MAX, (past + dt) // 4))
    attn_s = attn_s[:, :dt].reshape(ns, ATTN_WIDTH)
    y_sample = _finish(xs, pin_s, attn_s, wpool_bf, pscale_l, wout_bf,
                       ln2_l, w1_bf, w2_bf, lnf_l, ns).reshape(db, dt, d)
    new_k_sample = k_s.reshape(1, db, dt, N_HEADS, HEAD_DIM)
    new_v_sample = v_s.reshape(1, db, dt, N_HEADS, HEAD_DIM)
    new_kidx_sample = ki_s.reshape(1, db, dt, IDX_DIM)
    new_pool_sample = jnp.concatenate([hist, u_s3], axis=1)[:, -POOL_HIST:][None]

    return (y_prompt, y_sample, new_k_prompt, new_v_prompt, new_kidx_prompt, new_pool_prompt,
            new_k_sample, new_v_sample, new_kidx_sample, new_pool_sample)
```

```python
import functools

import jax
import jax.numpy as jnp
from jax import lax
from jax.experimental import pallas as pl
from jax.experimental.pallas import tpu as pltpu

D_MODEL = 1024
POOL_WIDTH = 512
POOL_WINDOWS = (2, 4, 8, 16)
POOL_GROUP = 128
POOL_HIST = 15
ATTN_WIDTH = 512
HEAD_DIM = 64
N_HEADS = 8
IDX_HEADS = 4
IDX_DIM = 64
TOPK_MAX = 256
PAGE_SIZE = 128
D_FF = 4096
NORM_EPS = 1e-6
NEG_INF = -1e30
IN_WIDTH = 2372
IN_PAD = 2432
KW_OFF = 2304

LANES = 128
SUBLANES = 8
ROW_TILE = 512
FINISH_TILE = 256
Q_TILE = LANES
KEY_CHUNK = 512
KEY_SUB = 512
HEAD_PAIRS = N_HEADS // 2
V_EXT = 2 * HEAD_DIM + 16
LOG2_E = 1.4426950408889634
PAGES_PER_STEP = 32
SAMPLE_ROWS = 8
VMEM_LIMIT = 56 * 1024 * 1024

F32 = jnp.float32
BF16 = jnp.bfloat16


def _rmsnorm(x, g):
    return x * lax.rsqrt(jnp.mean(x * x, axis=-1, keepdims=True) + NORM_EPS) * g


def _proj_split(x_ref, ln_ref, w_ref):
    xn = _rmsnorm(x_ref[...], ln_ref[...]).astype(BF16)
    z = jnp.dot(xn, w_ref[...], preferred_element_type=F32)
    return (z[:, 0:512], z[:, 512:1024] * (HEAD_DIM ** -0.5), z[:, 1024:1536], z[:, 1536:2048],
            z[:, 2048:KW_OFF] * (IDX_DIM ** -0.5), z[:, KW_OFF:IN_PAD])


def _proj_rows_kernel(x_ref, ln_ref, w_ref, u_ref, k_ref, v_ref, ki_ref, kw_ref, qb_ref, qib_ref):
    u, q, k, v, qi, kw = _proj_split(x_ref, ln_ref, w_ref)
    u_ref[...] = u
    k_ref[...] = k
    v_ref[...] = v
    ki_ref[...] = kw[:, :IDX_DIM]
    kw_ref[...] = kw
    qb_ref[...] = q.astype(BF16)
    qib_ref[...] = qi.astype(BF16)


def _proj_cols_kernel(x_ref, ln_ref, w_ref, u_ref, kt_ref, vt_ref, kit_ref,
                      kb_ref, kib_ref, qtb_ref, qitb_ref, vtb_ref, kwt_ref):
    u, q, k, v, qi, kw = _proj_split(x_ref, ln_ref, w_ref)
    u_ref[...] = u
    v_t = v.T
    kw_t = kw.T
    kt_ref[0] = k.T
    vt_ref[0] = v_t
    kit_ref[0] = kw_t[:IDX_DIM, :]
    kb_ref[...] = k.astype(BF16)
    kib_ref[...] = kw[:, :IDX_DIM].astype(BF16)
    qtb_ref[0] = (q * LOG2_E).T.astype(BF16)
    qitb_ref[0] = qi.T.astype(BF16)
    ones = jnp.ones((V_EXT - 2 * HEAD_DIM, v_t.shape[1]), F32)
    pieces = []
    for p in range(HEAD_PAIRS):
        pieces += [v_t[p * 2 * HEAD_DIM:(p + 1) * 2 * HEAD_DIM], ones]
    vtb_ref[0] = jnp.concatenate(pieces, axis=0).astype(BF16)
    kwt_ref[0] = kw_t


def _project(x2d, ln, w_bf, tm, seq_len=None):
    n = x2d.shape[0]
    nt = n // tm
    row = lambda w: pl.BlockSpec((tm, w), lambda i: (i, 0))
    qiw = IDX_HEADS * IDX_DIM
    sds = jax.ShapeDtypeStruct
    if seq_len is None:
        body = _proj_rows_kernel
        out_shape = (sds((n, POOL_WIDTH), F32), sds((n, ATTN_WIDTH), F32), sds((n, ATTN_WIDTH), F32),
                     sds((n, IDX_DIM), F32), sds((n, LANES), F32),
                     sds((n, ATTN_WIDTH), BF16), sds((n, qiw), BF16))
        out_specs = (row(POOL_WIDTH), row(ATTN_WIDTH), row(ATTN_WIDTH), row(IDX_DIM), row(LANES),
                     row(ATTN_WIDTH), row(qiw))
    else:
        body = _proj_cols_kernel
        tps = seq_len // tm
        nb = n // seq_len
        seq = lambda r: pl.BlockSpec((1, r, tm), lambda i: (i // tps, 0, i % tps))
        col = lambda r: pl.BlockSpec((1, r, tm), lambda i: (i, 0, 0))
        out_shape = (sds((n, POOL_WIDTH), F32),
                     sds((nb, ATTN_WIDTH, seq_len), F32),
                     sds((nb, ATTN_WIDTH, seq_len), F32),
                     sds((nb, IDX_DIM, seq_len), F32),
                     sds((n, ATTN_WIDTH), BF16), sds((n, IDX_DIM), BF16),
                     sds((nt, ATTN_WIDTH, tm), BF16),
                     sds((nt, qiw, tm), BF16),
                     sds((nt, HEAD_PAIRS * V_EXT, tm), BF16),
                     sds((nt, LANES, tm), F32))
        out_specs = (row(POOL_WIDTH), seq(ATTN_WIDTH), seq(ATTN_WIDTH), seq(IDX_DIM),
                     row(ATTN_WIDTH), row(IDX_DIM),
                     col(ATTN_WIDTH), col(qiw), col(HEAD_PAIRS * V_EXT), col(LANES))
    return pl.pallas_call(
        body,
        grid=(nt,),
        in_specs=[row(D_MODEL),
                  pl.BlockSpec((1, D_MODEL), lambda i: (0, 0)),
                  pl.BlockSpec((D_MODEL, IN_PAD), lambda i: (0, 0))],
        out_specs=out_specs,
        out_shape=out_shape,
        compiler_params=pltpu.CompilerParams(
            dimension_semantics=("arbitrary",), vmem_limit_bytes=VMEM_LIMIT),
        name="proj",
    )(x2d, ln, w_bf)


def _pool_prompt_kernel(u_ref, halo_ref, o_ref, ext_ref, *, tm, tiles_per_seq):
    it = pl.program_id(0) % tiles_per_seq
    ext_ref[0:16, :] = jnp.where(it == 0, 0.0, halo_ref[...])
    ext_ref[16:, :] = u_ref[...]
    pos = it * tm + lax.broadcasted_iota(jnp.int32, (tm, 1), 0)
    for g, w in enumerate(POOL_WINDOWS):
        cols = slice(g * POOL_GROUP, (g + 1) * POOL_GROUP)
        cur = ext_ref[16:16 + tm, cols]
        s = cur
        for j in range(1, w):
            s = s + ext_ref[16 - j:16 - j + tm, cols]
        cnt = jnp.minimum(pos + 1, w).astype(F32)
        o_ref[:, cols] = (s / cnt - cur).astype(BF16)


def _pool_prompt(u2d, seq_len, tm):
    n = u2d.shape[0]
    hb = tm // 16
    return pl.pallas_call(
        functools.partial(_pool_prompt_kernel, tm=tm, tiles_per_seq=seq_len // tm),
        grid=(n // tm,),
        in_specs=[pl.BlockSpec((tm, POOL_WIDTH), lambda i: (i, 0)),
                  pl.BlockSpec((16, POOL_WIDTH), lambda i: (jnp.maximum(i * hb - 1, 0), 0))],
        out_specs=pl.BlockSpec((tm, POOL_WIDTH), lambda i: (i, 0)),
        out_shape=jax.ShapeDtypeStruct((n, POOL_WIDTH), BF16),
        scratch_shapes=[pltpu.VMEM((tm + 16, POOL_WIDTH), F32)],
        compiler_params=pltpu.CompilerParams(dimension_semantics=("arbitrary",)),
        name="pool_prompt",
    )(u2d, u2d)


def _pool_sample_kernel(u_ref, h_ref, o_ref, *, start_pos):
    t = u_ref.shape[0]

    def row(r, cols):
        return h_ref[POOL_HIST + r, :, cols] if r < 0 else u_ref[r, :, cols]

    for i in range(t):
        for g, w in enumerate(POOL_WINDOWS):
            cols = slice(g * POOL_GROUP, (g + 1) * POOL_GROUP)
            cur = row(i, cols)
            s = cur
            for j in range(1, w):
                s = s + row(i - j, cols)
            cnt = float(min(start_pos + i + 1, w))
            o_ref[i, :, cols] = (s / cnt - cur).astype(BF16)


def _pool_sample(u_t, hist_t, start_pos):
    return pl.pallas_call(
        functools.partial(_pool_sample_kernel, start_pos=start_pos),
        out_shape=jax.ShapeDtypeStruct(u_t.shape, BF16),
        name="pool_sample",
    )(u_t, hist_t)


def _key_to_f32(key):
    bits = key ^ ((key >> 31) & 0x7FFFFFFF)
    return lax.bitcast_convert_type(bits, F32)


def _kth_largest_key(count_ge, shape, kf):
    def bit_step(it, st):
        tau, n_tau = st
        cand = tau ^ lax.shift_left(jnp.int32(1), 31 - it)
        cnt = count_ge(cand)
        ok = cnt >= kf
        return jnp.where(ok, cand, tau), jnp.where(ok, cnt, n_tau)

    return lax.fori_loop(0, 32, bit_step,
                         (jnp.full(shape, -2 ** 31, jnp.int32), jnp.full(shape, jnp.inf, F32)))


def _select_to_bias(s_ref, nch, ch, rows, real_rows, k, idx_bits):
    lane = lax.broadcasted_iota(jnp.int32, (rows, LANES), 1)
    kf = float(k)
    n_acc = min(4, ch)

    def count(pred):
        def body(c, accs):
            accs = list(accs)
            for u in range(ch):
                j = c * ch + u
                accs[u % n_acc] = accs[u % n_acc] + jnp.where(pred(s_ref[j], j), 1.0, 0.0)
            return tuple(accs)
        accs = lax.fori_loop(0, nch, body,
                             tuple(jnp.zeros((rows, LANES), F32) for _ in range(n_acc)))
        return jnp.sum(sum(accs[1:], accs[0]), axis=1, keepdims=True)

    def bcast(key):
        return jnp.broadcast_to(_key_to_f32(key), (rows, LANES))

    def count_ge(key):
        cb = bcast(key)
        return count(lambda blk, j: blk >= cb)

    tau, n_ge = _kth_largest_key(count_ge, (rows, 1), kf)
    tb = bcast(tau)
    need = kf - count(lambda blk, j: blk > tb)

    def index_search():
        def index_step(it, p):
            cand = p | lax.shift_left(jnp.int32(1), idx_bits - 1 - it)
            cb = jnp.broadcast_to(cand, (rows, LANES))
            cnt = count(lambda blk, j: (blk == tb) & ((lane + j * LANES) < cb))
            return jnp.where(cnt < need, cand, p)
        return lax.fori_loop(0, idx_bits, index_step, jnp.zeros((rows, 1), jnp.int32))

    real = lax.broadcasted_iota(jnp.int32, (rows, 1), 0) < real_rows
    any_tie = jnp.max(jnp.where((n_ge > kf) & real, 1.0, 0.0)) > 0.0
    cut = lax.cond(any_tie, index_search,
                   lambda: jnp.full((rows, 1), 2 ** 30, jnp.int32))
    cutb = jnp.broadcast_to(cut, (rows, LANES))

    def write(c, carry):
        for u in range(ch):
            j = c * ch + u
            blk = s_ref[j]
            keep = (blk > tb) | ((blk == tb) & ((lane + j * LANES) <= cutb))
            s_ref[j] = jnp.where(keep, 0.0, NEG_INF)
        return carry
    lax.fori_loop(0, nch, write, 0)


def _select_to_bias_t(s_ref, tri_ref, nch, k):
    ck = s_ref.shape[1]
    kf = float(k)

    n_acc = 8
    rows_per_acc = ck // n_acc

    def count(pred):
        def body(c, accs):
            ind = jnp.where(pred(s_ref[c]), 1.0, 0.0)
            out = []
            for a in range(n_acc):
                part = ind[a * rows_per_acc:(a + 1) * rows_per_acc]
                out.append(accs[a] + jnp.sum(
                    part.reshape(rows_per_acc // SUBLANES, SUBLANES, LANES), axis=0))
            return tuple(out)
        accs = lax.fori_loop(0, nch, body,
                             tuple(jnp.zeros((SUBLANES, LANES), F32) for _ in range(n_acc)))
        return jnp.sum(sum(accs[1:], accs[0]), axis=0, keepdims=True)

    def count_ge(key):
        cf = _key_to_f32(key)
        return count(lambda blk: blk >= cf)

    tau, n_ge = _kth_largest_key(count_ge, (1, LANES), kf)
    tf = _key_to_f32(tau)
    excess = n_ge - kf

    def write(i, seen):
        c = nch - 1 - i
        blk = s_ref[c]
        tied = blk == tf
        after = seen + jnp.dot(tri_ref[...], jnp.where(tied, 1.0, 0.0).astype(BF16),
                               preferred_element_type=F32)
        keep = (blk > tf) | (tied & (after > excess))
        s_ref[c] = jnp.where(keep, 0.0, NEG_INF)
        return after[0:1, :]
    lax.fori_loop(0, nch, write, jnp.zeros((1, LANES), F32))


_NT = (((1,), (1,)), ((), ()))


def _softmax_finish(m_ref, l_ref, acc_ref):
    return jnp.concatenate(
        [acc_ref[:, h * HEAD_DIM:(h + 1) * HEAD_DIM] / l_ref[h] for h in range(N_HEADS)], axis=1)


def _attn_prompt_kernel(qt_ref, qit_ref, kwt_ref, kib_ref, kb_ref, vt_ref, tri_ref, o_ref,
                        s_ref, wq_ref, m_ref, acc_ref, *, topk):
    qb = pl.program_id(1)
    nch = qb // (KEY_CHUNK // Q_TILE) + 1
    q_pos = qb * Q_TILE + lax.broadcasted_iota(jnp.int32, (KEY_CHUNK, LANES), 1)
    key_row = lax.broadcasted_iota(jnp.int32, (KEY_CHUNK, LANES), 0)

    qit = qit_ref[0]
    w_rows = [kwt_ref[0, IDX_DIM + h:IDX_DIM + h + 1, :] * (IDX_HEADS ** -0.5)
              for h in range(IDX_HEADS)]
    wqi = [jnp.concatenate([qit[(2 * p) * IDX_DIM:(2 * p + 1) * IDX_DIM, :],
                            qit[(2 * p + 1) * IDX_DIM:(2 * p + 2) * IDX_DIM, :]], axis=1)
           for p in range(IDX_HEADS // 2)]

    def score_chunk(c, carry):
        ki = kib_ref[0, c]
        acc = jnp.zeros((KEY_CHUNK, LANES), F32)
        rs = [jnp.dot(ki, wqi[p], preferred_element_type=F32) for p in range(IDX_HEADS // 2)]
        for p, r in enumerate(rs):
            acc = acc + w_rows[2 * p] * jnp.maximum(r[:, :LANES], 0.0)
            acc = acc + w_rows[2 * p + 1] * jnp.maximum(r[:, LANES:], 0.0)
        s_ref[c] = jnp.where(key_row + c * KEY_CHUNK <= q_pos, acc, NEG_INF)
        return carry
    lax.fori_loop(0, nch, score_chunk, 0)

    @pl.when((qb + 1) * Q_TILE <= topk)
    def _():
        def write(c, carry):
            s_ref[c] = jnp.where(key_row + c * KEY_CHUNK <= q_pos, 0.0, NEG_INF)
            return carry
        lax.fori_loop(0, nch, write, 0)

    @pl.when((qb + 1) * Q_TILE > topk)
    def _():
        _select_to_bias_t(s_ref, tri_ref, nch, topk)

    half = lax.broadcasted_iota(jnp.int32, (2 * HEAD_DIM, LANES), 0) < HEAD_DIM
    for p in range(HEAD_PAIRS):
        qp = qt_ref[0, p * 2 * HEAD_DIM:(p + 1) * 2 * HEAD_DIM, :]
        zero = jnp.zeros_like(qp)
        wq_ref[p] = jnp.concatenate([jnp.where(half, qp, zero), jnp.where(half, zero, qp)], axis=1)
    m_ref[...] = jnp.full(m_ref.shape, -jnp.inf, F32)
    acc_ref[...] = jnp.zeros(acc_ref.shape, F32)

    blocks = [(sub, p) for sub in range(KEY_CHUNK // KEY_SUB) for p in range(HEAD_PAIRS)]
    keys_of = lambda sub: slice(sub * KEY_SUB, (sub + 1) * KEY_SUB)
    dims_of = lambda p: slice(p * 2 * HEAD_DIM, (p + 1) * 2 * HEAD_DIM)
    vrows_of = lambda p: slice(p * V_EXT, (p + 1) * V_EXT)

    def logits(c, sub, p):
        return jnp.dot(kb_ref[0, c, keys_of(sub), dims_of(p)], wq_ref[p],
                       preferred_element_type=F32)

    def attn_chunk(c, carry):
        st_next = logits(c, *blocks[0])
        pending = None
        for i, (sub, p) in enumerate(blocks):
            st = st_next
            if i + 1 < len(blocks):
                st_next = logits(c, *blocks[i + 1])
            bias = s_ref[c, keys_of(sub), :]
            probs, alphas = [], []
            for hh in range(2):
                h = 2 * p + hh
                s = st[:, hh * LANES:(hh + 1) * LANES] + bias
                m_old = m_ref[h]
                m_new = jnp.maximum(m_old, jnp.max(s, axis=0, keepdims=True))
                alphas.append(jnp.exp2(m_old - m_new))
                probs.append(jnp.exp2(s - m_new).astype(BF16))
                m_ref[h] = m_new
            pv = jnp.dot(vt_ref[0, c, vrows_of(p), keys_of(sub)], jnp.concatenate(probs, axis=1),
                         preferred_element_type=F32)
            if pending is not None:
                pp, pa, ppv = pending
                acc_ref[pp] = acc_ref[pp] * pa + ppv
            pending = (p, jnp.concatenate(alphas, axis=1), pv)
        pp, pa, ppv = pending
        acc_ref[pp] = acc_ref[pp] * pa + ppv
        return carry
    lax.fori_loop(0, nch, attn_chunk, 0)

    outs = []
    for h in range(N_HEADS):
        p, hh = divmod(h, 2)
        lanes = slice(hh * LANES, (hh + 1) * LANES)
        o_t = acc_ref[p, hh * HEAD_DIM:(hh + 1) * HEAD_DIM, lanes]
        outs.append(o_t / acc_ref[p, 2 * HEAD_DIM:2 * HEAD_DIM + 1, lanes])
    o_ref[0] = jnp.concatenate(outs, axis=0).T.astype(BF16)


def _attn_prompt(qt, qit, kwt, kib, kb, vt, b, t, topk):
    nc = t // KEY_CHUNK
    tiles_per_seq = t // ROW_TILE
    qpt = ROW_TILE // Q_TILE
    assert t % KEY_CHUNK == 0 and t <= 2 ** 13 and KEY_CHUNK == ROW_TILE
    tcol = lambda r: pl.BlockSpec((1, r, Q_TILE),
                                  lambda i, j: (i * tiles_per_seq + j // qpt, 0, j % qpt))
    full = lambda r, c: pl.BlockSpec((1, nc, r, c), lambda i, j: (i, 0, 0, 0))
    return pl.pallas_call(
        functools.partial(_attn_prompt_kernel, topk=topk),
        grid=(b, t // Q_TILE),
        in_specs=[tcol(ATTN_WIDTH), tcol(IDX_HEADS * IDX_DIM), tcol(LANES),
                  full(KEY_CHUNK, IDX_DIM), full(KEY_CHUNK, ATTN_WIDTH),
                  full(HEAD_PAIRS * V_EXT, KEY_CHUNK),
                  pl.BlockSpec((KEY_CHUNK, KEY_CHUNK), lambda i, j: (0, 0))],
        out_specs=pl.BlockSpec((1, Q_TILE, ATTN_WIDTH), lambda i, j: (i, j, 0)),
        out_shape=jax.ShapeDtypeStruct((b, t, ATTN_WIDTH), BF16),
        scratch_shapes=[pltpu.VMEM((nc, KEY_CHUNK, LANES), F32),
                        pltpu.VMEM((HEAD_PAIRS, 2 * HEAD_DIM, 2 * LANES), BF16),
                        pltpu.VMEM((N_HEADS, 1, LANES), F32),
                        pltpu.VMEM((HEAD_PAIRS, V_EXT, 2 * LANES), F32)],
        compiler_params=pltpu.CompilerParams(
            dimension_semantics=("arbitrary", "arbitrary"), vmem_limit_bytes=VMEM_LIMIT),
        name="attn_prompt",
    )(qt, qit, kwt, kib, kb, vt, jnp.triu(jnp.ones((KEY_CHUNK, KEY_CHUNK), BF16)))


def _select_sample_kernel(pt_ref, qi_ref, kw_ref, kin_ref, *rest, n_steps, real_rows, topk):
    pages = rest[:PAGES_PER_STEP]
    o_ref, s_ref, kbuf_ref = rest[PAGES_PER_STEP:]
    g = pl.program_id(1)
    rows = SAMPLE_ROWS
    qi = qi_ref[0].astype(BF16)
    kw = kw_ref[0]
    w_cols = [kw[:, IDX_DIM + h:IDX_DIM + h + 1] * (IDX_HEADS ** -0.5) for h in range(IDX_HEADS)]

    def scores(keys_t):
        acc = jnp.zeros((rows, keys_t.shape[1]), F32)
        ss = [jnp.dot(qi[:, h * IDX_DIM:(h + 1) * IDX_DIM], keys_t, preferred_element_type=F32)
              for h in range(IDX_HEADS)]
        for h in range(IDX_HEADS):
            acc = acc + w_cols[h] * jnp.maximum(ss[h], 0.0)
        return acc

    for j in range(PAGES_PER_STEP):
        kbuf_ref[:, j * PAGE_SIZE:(j + 1) * PAGE_SIZE] = pages[j][0].astype(BF16)
    past = scores(kbuf_ref[...])
    for j in range(PAGES_PER_STEP):
        s_ref[g * PAGES_PER_STEP + j] = past[:, j * PAGE_SIZE:(j + 1) * PAGE_SIZE]

    n_past = n_steps * PAGES_PER_STEP

    @pl.when(g == 0)
    def _():
        new = scores(kin_ref[0].astype(BF16))
        lane = lax.broadcasted_iota(jnp.int32, (rows, LANES), 1)
        row = lax.broadcasted_iota(jnp.int32, (rows, LANES), 0)
        s_ref[n_past] = jnp.where(lane <= row, new, NEG_INF)

    @pl.when(g == n_steps - 1)
    def _():
        _select_to_bias(s_ref, 1, n_past + 1, rows, real_rows, topk, 14)
        o_ref[0] = s_ref[...]


def _attn_sample_kernel(pt_ref, q_ref, b_ref, kn_ref, vn_ref, *rest):
    kpages = rest[:PAGES_PER_STEP]
    vpages = rest[PAGES_PER_STEP:2 * PAGES_PER_STEP]
    o_ref, kbuf_ref, vbuf_ref, m_ref, l_ref, acc_ref = rest[2 * PAGES_PER_STEP:]
    g = pl.program_id(1)
    n_steps = pl.num_programs(1)
    n_past = b_ref.shape[1] - 1
    q = q_ref[0].astype(BF16)

    def update(k_of, v_of, bias):
        cols = [slice(h * HEAD_DIM, (h + 1) * HEAD_DIM) for h in range(N_HEADS)]
        logits = [jnp.dot(q[:, cols[h]], k_of(h), preferred_element_type=F32) + bias
                  for h in range(N_HEADS)]
        probs, alphas = [], []
        for h in range(N_HEADS):
            m_old = m_ref[h]
            m_new = jnp.maximum(m_old, jnp.max(logits[h], axis=1, keepdims=True))
            alpha = jnp.exp(m_old - m_new)
            p = jnp.exp(logits[h] - m_new)
            l_ref[h] = alpha * l_ref[h] + jnp.sum(p, axis=1, keepdims=True)
            m_ref[h] = m_new
            probs.append(p.astype(BF16))
            alphas.append(alpha)
        pvs = [lax.dot_general(probs[h], v_of(h), _NT, preferred_element_type=F32)
               for h in range(N_HEADS)]
        for h in range(N_HEADS):
            acc_ref[:, cols[h]] = alphas[h] * acc_ref[:, cols[h]] + pvs[h]

    @pl.when(g == 0)
    def _():
        m_ref[...] = jnp.full(m_ref.shape, -jnp.inf, F32)
        l_ref[...] = jnp.zeros(l_ref.shape, F32)
        acc_ref[...] = jnp.zeros(acc_ref.shape, F32)
        update(lambda h: kn_ref[0, h].astype(BF16), lambda h: vn_ref[0, h].astype(BF16),
               b_ref[0, n_past])

    for j in range(PAGES_PER_STEP):
        keys = slice(j * PAGE_SIZE, (j + 1) * PAGE_SIZE)
        for h in range(N_HEADS):
            kbuf_ref[h, :, keys] = kpages[j][0, h].astype(BF16)
            vbuf_ref[h, :, keys] = vpages[j][0, h].astype(BF16)
    bias = jnp.concatenate([b_ref[0, g * PAGES_PER_STEP + j] for j in range(PAGES_PER_STEP)], axis=1)
    update(lambda h: kbuf_ref[h], lambda h: vbuf_ref[h], bias)

    @pl.when(g == n_steps - 1)
    def _():
        o_ref[0] = _softmax_finish(m_ref, l_ref, acc_ref)


def _page_specs(block):
    zeros = (0,) * (len(block) - 1)

    def spec(j):
        return pl.BlockSpec(block, lambda b, g, pt: (pt[b, g * PAGES_PER_STEP + j],) + zeros)
    return [spec(j) for j in range(PAGES_PER_STEP)]


def _attn_sample(page_table, q_s, qi_s, kw_s, kin_t, kn_t, vn_t, k_pages, v_pages, ki_pages,
                 real_rows, topk):
    db, n_pages = page_table.shape
    assert n_pages % PAGES_PER_STEP == 0
    n_steps = n_pages // PAGES_PER_STEP
    nblk = n_pages + 1
    keys_per_step = PAGES_PER_STEP * PAGE_SIZE
    per_seq = lambda *blk: pl.BlockSpec((1,) + blk, lambda b, g, pt: (b,) + (0,) * len(blk))
    bias = pl.pallas_call(
        functools.partial(_select_sample_kernel, n_steps=n_steps, real_rows=real_rows, topk=topk),
        grid_spec=pltpu.PrefetchScalarGridSpec(
            num_scalar_prefetch=1,
            grid=(db, n_steps),
            in_specs=[per_seq(SAMPLE_ROWS, IDX_HEADS * IDX_DIM), per_seq(SAMPLE_ROWS, LANES),
                      per_seq(IDX_DIM, PAGE_SIZE)] + _page_specs((1, IDX_DIM, PAGE_SIZE)),
            out_specs=per_seq(nblk, SAMPLE_ROWS, LANES),
            scratch_shapes=[pltpu.VMEM((nblk, SAMPLE_ROWS, LANES), F32),
                            pltpu.VMEM((IDX_DIM, keys_per_step), BF16)]),
        out_shape=jax.ShapeDtypeStruct((db, nblk, SAMPLE_ROWS, LANES), F32),
        compiler_params=pltpu.CompilerParams(dimension_semantics=("arbitrary", "arbitrary")),
        name="select_sample",
    )(page_table, qi_s, kw_s, kin_t, *([ki_pages] * PAGES_PER_STEP))

    head_page = (1, N_HEADS, HEAD_DIM, PAGE_SIZE)
    return pl.pallas_call(
        _attn_sample_kernel,
        grid_spec=pltpu.PrefetchScalarGridSpec(
            num_scalar_prefetch=1,
            grid=(db, n_steps),
            in_specs=[per_seq(SAMPLE_ROWS, ATTN_WIDTH), per_seq(nblk, SAMPLE_ROWS, LANES),
                      per_seq(N_HEADS, HEAD_DIM, PAGE_SIZE), per_seq(N_HEADS, HEAD_DIM, PAGE_SIZE)]
            + _page_specs(head_page) + _page_specs(head_page),
            out_specs=per_seq(SAMPLE_ROWS, ATTN_WIDTH),
            scratch_shapes=[pltpu.VMEM((N_HEADS, HEAD_DIM, keys_per_step), BF16),
                            pltpu.VMEM((N_HEADS, HEAD_DIM, keys_per_step), BF16),
                            pltpu.VMEM((N_HEADS, SAMPLE_ROWS, 1), F32),
                            pltpu.VMEM((N_HEADS, SAMPLE_ROWS, 1), F32),
                            pltpu.VMEM((SAMPLE_ROWS, ATTN_WIDTH), F32)]),
        out_shape=jax.ShapeDtypeStruct((db, SAMPLE_ROWS, ATTN_WIDTH), F32),
        compiler_params=pltpu.CompilerParams(
            dimension_semantics=("arbitrary", "arbitrary"), vmem_limit_bytes=VMEM_LIMIT),
        name="attn_sample",
    )(page_table, q_s, bias, kn_t, vn_t,
      *([k_pages] * PAGES_PER_STEP), *([v_pages] * PAGES_PER_STEP))


def _finish_kernel(x_ref, pin_ref, attn_ref, wpool_ref, pscale_ref, wout_ref, ln2_ref,
                   w1_ref, w2_ref, lnf_ref, o_ref):
    pool = jnp.concatenate(
        [jnp.dot(pin_ref[:, g * POOL_GROUP:(g + 1) * POOL_GROUP], wpool_ref[g],
                 preferred_element_type=F32) for g in range(len(POOL_WINDOWS))], axis=1)
    pool = (pool * pscale_ref[...]).astype(BF16)
    mixed = jnp.concatenate([pool, attn_ref[...].astype(BF16)], axis=1)
    h = x_ref[...] + jnp.dot(mixed, wout_ref[...], preferred_element_type=F32)
    hn = _rmsnorm(h, ln2_ref[...]).astype(BF16)
    y = h
    fc = 1024
    for c in range(D_FF // fc):
        a = jnp.dot(hn, w1_ref[:, c * fc:(c + 1) * fc], preferred_element_type=F32)
        a = jnp.square(jnp.maximum(a, 0.0)).astype(BF16)
        y = y + jnp.dot(a, w2_ref[c * fc:(c + 1) * fc, :], preferred_element_type=F32)
    o_ref[...] = _rmsnorm(y, lnf_ref[...])


def _finish(x2d, pin, attn, wpool, pscale, wout, ln2, w1, w2, lnf, tm):
    n = x2d.shape[0]
    row = lambda w: pl.BlockSpec((tm, w), lambda i: (i, 0))
    const = lambda shape: pl.BlockSpec(shape, lambda i: (0,) * len(shape))
    return pl.pallas_call(
        _finish_kernel,
        grid=(n // tm,),
        in_specs=[row(D_MODEL), row(POOL_WIDTH), row(ATTN_WIDTH),
                  const(wpool.shape), const(pscale.shape), const(wout.shape), const(ln2.shape),
                  const(w1.shape), const(w2.shape), const(lnf.shape)],
        out_specs=row(D_MODEL),
        out_shape=jax.ShapeDtypeStruct((n, D_MODEL), F32),
        compiler_params=pltpu.CompilerParams(
            dimension_semantics=("arbitrary",), vmem_limit_bytes=VMEM_LIMIT),
        name="finish",
    )(x2d, pin, attn, wpool, pscale, wout, ln2, w1, w2, lnf)


def kernel(x_prompt, x_sample, cache_k, cache_v, cache_kidx, state_pool, page_table,
           ln1, w_in, w_pool, pool_scale, w_out, ln2, w_ff1, w_ff2, ln_f):
    depth = ln1.shape[0]
    assert depth == 1
    b, t, d = x_prompt.shape
    db, dt, _ = x_sample.shape
    n_pool = cache_k.shape[1]
    past = page_table.shape[1] * PAGE_SIZE
    l = 0

    w_in_bf = jnp.pad(w_in[l], ((0, 0), (0, IN_PAD - IN_WIDTH))).astype(BF16)
    wpool_bf = w_pool[l].astype(BF16)
    wout_bf = w_out[l].astype(BF16)
    w1_bf = w_ff1[l].astype(BF16)
    w2_bf = w_ff2[l].astype(BF16)
    ln1_l = ln1[l].reshape(1, d)
    ln2_l = ln2[l].reshape(1, d)
    lnf_l = ln_f.reshape(1, d)
    pscale_l = pool_scale[l].reshape(1, POOL_WIDTH)

    xp = x_prompt.reshape(b * t, d)
    u, k_t, v_t, ki_t, kb, kib, qt, qit, vtb, kwt = _project(xp, ln1_l, w_in_bf, ROW_TILE, seq_len=t)
    pin = _pool_prompt(u, t, ROW_TILE)
    nc = t // KEY_CHUNK
    attn = _attn_prompt(
        qt, qit, kwt, kib.reshape(b, nc, KEY_CHUNK, IDX_DIM),
        kb.reshape(b, nc, KEY_CHUNK, ATTN_WIDTH), vtb.reshape(b, nc, HEAD_PAIRS * V_EXT, KEY_CHUNK),
        b, t, min(TOPK_MAX, t // 4))
    y_prompt = _finish(xp, pin, attn.reshape(b * t, ATTN_WIDTH), wpool_bf, pscale_l, wout_bf,
                       ln2_l, w1_bf, w2_bf, lnf_l, FINISH_TILE).reshape(b, t, d)
    new_k_prompt = jnp.transpose(k_t.reshape(b, N_HEADS, HEAD_DIM, t), (0, 3, 1, 2))[None]
    new_v_prompt = jnp.transpose(v_t.reshape(b, N_HEADS, HEAD_DIM, t), (0, 3, 1, 2))[None]
    new_kidx_prompt = jnp.transpose(ki_t, (0, 2, 1))[None]
    new_pool_prompt = u.reshape(b, t, POOL_WIDTH)[:, t - POOL_HIST:][None]

    ns = db * dt
    xs = x_sample.reshape(ns, d)
    u_s, k_s, v_s, ki_s, kw_s, qb_s, qib_s = _project(xs, ln1_l, w_in_bf, ns)
    hist = state_pool[l]
    u_s3 = u_s.reshape(db, dt, POOL_WIDTH)
    pin_s = _pool_sample(jnp.swapaxes(u_s3, 0, 1), jnp.swapaxes(hist, 0, 1), past)
    pin_s = jnp.swapaxes(pin_s, 0, 1).reshape(ns, POOL_WIDTH)

    def pad_rows(a, rows):
        a = a.reshape(db, dt, a.shape[-1]).astype(F32)
        return jnp.pad(a, ((0, 0), (0, rows - dt), (0, 0)))

    def tokens_last(a, *mid):
        a = jnp.moveaxis(a.reshape((db, dt) + mid), 1, -1)
        return jnp.pad(a, ((0, 0),) * (a.ndim - 1) + ((0, PAGE_SIZE - dt),))

    attn_s = _attn_sample(
        page_table,
        pad_rows(qb_s, SAMPLE_ROWS),
        pad_rows(qib_s, SAMPLE_ROWS),
        pad_rows(kw_s, SAMPLE_ROWS),
        tokens_last(ki_s, IDX_DIM), tokens_last(k_s, N_HEADS, HEAD_DIM),
        tokens_last(v_s, N_HEADS, HEAD_DIM),
        jnp.transpose(cache_k[l], (0, 2, 3, 1)), jnp.transpose(cache_v[l], (0, 2, 3, 1)),
        jnp.transpose(cache_kidx[l], (0, 2, 1)), dt, min(TOPK_MAX, (past + dt) // 4))
    attn_s = attn_s[:, :dt].reshape(ns, ATTN_WIDTH)
    y_sample = _finish(xs, pin_s, attn_s, wpool_bf, pscale_l, wout_bf,
                       ln2_l, w1_bf, w2_bf, lnf_l, ns).reshape(db, dt, d)
    new_k_sample = k_s.reshape(1, db, dt, N_HEADS, HEAD_DIM)
    new_v_sample = v_s.reshape(1, db, dt, N_HEADS, HEAD_DIM)
    new_kidx_sample = ki_s.reshape(1, db, dt, IDX_DIM)
    new_pool_sample = jnp.concatenate([hist, u_s3], axis=1)[:, -POOL_HIST:][None]

    return (y_prompt, y_sample, new_k_prompt, new_v_prompt, new_kidx_prompt, new_pool_prompt,
            new_k_sample, new_v_sample, new_kidx_sample, new_pool_sample)
```
